```python
import math
import jax, jax.numpy as jnp
from jax import lax
import numpy as np

D_MODEL = 2048
BATCH = 8
SEQ = 2048
DEPTH = 2

N_A_LAYERS = DEPTH // 2
N_B_LAYERS = DEPTH - N_A_LAYERS
SSM_GROUP = 16
SSM_GROUPS = D_MODEL // SSM_GROUP
SSM_STATE = 64
SCAN_CHUNK = 128
DT_MIN = 1e-3
DT_MAX = 1e-1
HEAD_DIM = 128
N_HEADS = D_MODEL // (2 * HEAD_DIM)
D_FF = 4 * D_MODEL
ROPE_THETA = 10000.0
Q_BLOCK = 128
EPS = 1e-6
LAMBDA_STD = 0.1

kernel_name = 's5_diffattn_yoco_hybrid'


def rms_norm(x, g):
    xf = x.astype(jnp.float32)
    y = xf * lax.rsqrt(jnp.mean(xf * xf, axis=-1, keepdims=True) + EPS)
    return (y * g.astype(jnp.float32)).astype(x.dtype)


def rope_tables(L):
    pos = jnp.arange(L, dtype=jnp.float32)
    inv_freq = 1.0 / (ROPE_THETA ** (jnp.arange(0, HEAD_DIM, 2, dtype=jnp.float32) / HEAD_DIM))
    ang = pos[:, None] * inv_freq[None, :]
    emb = jnp.concatenate([ang, ang], axis=-1)
    return jnp.cos(emb)[:, None, :], jnp.sin(emb)[:, None, :]


def apply_rope(t, cos, sin):
    tf = t.astype(jnp.float32)
    t1, t2 = jnp.split(tf, 2, axis=-1)
    rot = jnp.concatenate([-t2, t1], axis=-1)
    return (tf * cos + rot * sin).astype(t.dtype)


def _cmul(ar, ai, br, bi):
    return ar * br - ai * bi, ar * bi + ai * br


def _scan_combine(e1, e2):
    a1r, a1i, b1r, b1i = e1
    a2r, a2i, b2r, b2i = e2
    ar, ai = _cmul(a2r, a2i, a1r, a1i)
    br, bi = _cmul(a2r, a2i, b1r, b1i)
    return ar, ai, br + b2r, bi + b2i


def s5_mixer(x, w_in, a_re, a_im, log_dt, b_re, b_im, c_re, c_im, d_skip, w_glu):
    f32 = jnp.float32
    Bsz, L, _ = x.shape
    u = (x @ w_in).astype(f32).reshape(Bsz, L, SSM_GROUPS, SSM_GROUP)
    step = jnp.exp(log_dt.astype(f32))[:, None]
    lam_re = jnp.minimum(a_re.astype(f32), -1e-4)
    lam_im = a_im.astype(f32)
    mag = jnp.exp(step * lam_re)
    abar_re = mag * jnp.cos(step * lam_im)
    abar_im = mag * jnp.sin(step * lam_im)
    den = lam_re * lam_re + lam_im * lam_im
    nr = abar_re - 1.0
    ni = abar_im
    coef_re = (nr * lam_re + ni * lam_im) / den
    coef_im = (ni * lam_re - nr * lam_im) / den
    bbar_re, bbar_im = _cmul(coef_re[..., None], coef_im[..., None], b_re.astype(f32), b_im.astype(f32))
    c_r = c_re.astype(f32)
    c_i = c_im.astype(f32)

    n_chunks = L // SCAN_CHUNK
    u_chunks = u.reshape(Bsz, n_chunks, SCAN_CHUNK, SSM_GROUPS, SSM_GROUP).transpose(1, 0, 2, 3, 4)

    def chunk_step(carry, u_c):
        s_re, s_im = carry
        bu_re = jnp.einsum('btgp,gnp->btgn', u_c, bbar_re)
        bu_im = jnp.einsum('btgp,gnp->btgn', u_c, bbar_im)
        a_r = jnp.broadcast_to(abar_re, bu_re.shape)
        a_i = jnp.broadcast_to(abar_im, bu_re.shape)
        acum_re, acum_im, h_re, h_im = lax.associative_scan(
            _scan_combine, (a_r, a_i, bu_re, bu_im), axis=1)
        cr, ci = _cmul(acum_re, acum_im, s_re[:, None], s_im[:, None])
        st_re = h_re + cr
        st_im = h_im + ci
        y = jnp.einsum('btgn,gpn->btgp', st_re, c_r) - jnp.einsum('btgn,gpn->btgp', st_im, c_i)
        return (st_re[:, -1], st_im[:, -1]), y

    init = (jnp.zeros((Bsz, SSM_GROUPS, SSM_STATE), f32), jnp.zeros((Bsz, SSM_GROUPS, SSM_STATE), f32))
    _, ys = lax.scan(chunk_step, init, u_chunks)
    y = ys.transpose(1, 0, 2, 3, 4).reshape(Bsz, L, D_MODEL)
    y = y + d_skip.astype(f32) * u.reshape(Bsz, L, D_MODEL)
    z = jax.nn.gelu(y).astype(x.dtype)
    val, gate = jnp.split(z @ w_glu, 2, axis=-1)
    return val * jax.nn.sigmoid(gate)


def shared_kv(h, g_kv, w_kv, cos, sin):
    Bsz, L, _ = h.shape
    kv = rms_norm(h, g_kv) @ w_kv
    k, v = jnp.split(kv, 2, axis=-1)
    k = k.reshape(Bsz, L, N_HEADS, 2, HEAD_DIM)
    k1 = apply_rope(k[..., 0, :], cos, sin)
    k2 = apply_rope(k[..., 1, :], cos, sin)
    v = v.reshape(Bsz, L, N_HEADS, 2 * HEAD_DIM)
    return k1, k2, v


def diff_attention(x, w_q, lq1, lk1, lq2, lk2, g_sub, w_o, k1, k2, v, cos, sin, lambda_init):
    f32 = jnp.float32
    Bsz, L, _ = x.shape
    scale = HEAD_DIM ** -0.5
    q = (x @ w_q).reshape(Bsz, L, N_HEADS, 2, HEAD_DIM)
    q1 = apply_rope(q[..., 0, :], cos, sin) * scale
    q2 = apply_rope(q[..., 1, :], cos, sin) * scale
    lam = (jnp.exp(jnp.sum(lq1.astype(f32) * lk1.astype(f32)))
           - jnp.exp(jnp.sum(lq2.astype(f32) * lk2.astype(f32))) + lambda_init)

    outs = []
    for i in range(L // Q_BLOCK):
        s0, e = i * Q_BLOCK, (i + 1) * Q_BLOCK
        causal = jnp.arange(e)[None, :] <= (s0 + jnp.arange(Q_BLOCK))[:, None]

        def probs(qb, kp):
            sc = jnp.einsum('bqhd,bkhd->bhqk', qb, kp).astype(f32)
            sc = jnp.where(causal, sc, -jnp.inf)
            return jax.nn.softmax(sc, axis=-1)

        p = probs(q1[:, s0:e], k1[:, :e]) - lam * probs(q2[:, s0:e], k2[:, :e])
        outs.append(jnp.einsum('bhqk,bkhe->bqhe', p.astype(v.dtype), v[:, :e]))
    o = jnp.concatenate(outs, axis=1)
    o = rms_norm(o, g_sub) * (1.0 - lambda_init)
    return o.reshape(Bsz, L, D_MODEL) @ w_o


def sq_relu_mlp(x, w_up, w_down):
    return jnp.square(jax.nn.relu(x @ w_up)) @ w_down


def setup_inputs(seed: int = 0) -> dict:
    key = jax.random.key(seed)
    ks = jax.random.split(key, 32)
    f32 = jnp.float32
    D, G, N, P = D_MODEL, SSM_GROUPS, SSM_STATE, SSM_GROUP

    def nrm(k, shape, std):
        return jax.random.normal(k, shape, f32) * std

    def gain(k, shape):
        return 1.0 + 0.02 * jax.random.normal(k, shape, f32)

    n_idx = jnp.arange(N, dtype=f32)
    return {
        'x': jax.random.normal(ks[0], (BATCH, SEQ, D), f32),
        'mix_pre_g': gain(ks[1], (DEPTH, D)),
        'mix_post_g': gain(ks[2], (DEPTH, D)),
        'mlp_pre_g': gain(ks[3], (DEPTH, D)),
        'mlp_post_g': gain(ks[4], (DEPTH, D)),
        'ssm_w_in': nrm(ks[5], (N_A_LAYERS, D, D), D ** -0.5),
        'ssm_a_re': -0.5 + 0.01 * jax.random.normal(ks[6], (N_A_LAYERS, G, N), f32),
        'ssm_a_im': math.pi * n_idx + 0.01 * jax.random.normal(ks[7], (N_A_LAYERS, G, N), f32),
        'ssm_log_dt': jax.random.uniform(ks[8], (N_A_LAYERS, G), f32, math.log(DT_MIN), math.log(DT_MAX)),
        'ssm_b_re': nrm(ks[9], (N_A_LAYERS, G, N, P), (0.5 / P) ** 0.5),
        'ssm_b_im': nrm(ks[10], (N_A_LAYERS, G, N, P), (0.5 / P) ** 0.5),
        'ssm_c_re': nrm(ks[11], (N_A_LAYERS, G, P, N), (0.5 / N) ** 0.5),
        'ssm_c_im': nrm(ks[12], (N_A_LAYERS, G, P, N), (0.5 / N) ** 0.5),
        'ssm_d': nrm(ks[13], (N_A_LAYERS, D), 1.0),
        'ssm_w_glu': nrm(ks[14], (N_A_LAYERS, D, 2 * D), D ** -0.5),
        'kv_norm_g': gain(ks[15], (D,)),
        'w_kv': nrm(ks[16], (D, 2 * D), D ** -0.5),
        'attn_w_q': nrm(ks[17], (N_B_LAYERS, D, D), D ** -0.5),
        'lam_q1': nrm(ks[18], (N_B_LAYERS, HEAD_DIM), LAMBDA_STD),
        'lam_k1': nrm(ks[19], (N_B_LAYERS, HEAD_DIM), LAMBDA_STD),
        'lam_q2': nrm(ks[20], (N_B_LAYERS, HEAD_DIM), LAMBDA_STD),
        'lam_k2': nrm(ks[21], (N_B_LAYERS, HEAD_DIM), LAMBDA_STD),
        'attn_subln_g': gain(ks[22], (N_B_LAYERS, 2 * HEAD_DIM)),
        'attn_w_o': nrm(ks[23], (N_B_LAYERS, D, D), D ** -0.5),
        'mlp_w_up': nrm(ks[24], (DEPTH, D, D_FF), D ** -0.5),
        'mlp_w_down': nrm(ks[25], (DEPTH, D_FF, D), D_FF ** -0.5),
    }


def reference(x, mix_pre_g, mix_post_g, mlp_pre_g, mlp_post_g,
              ssm_w_in, ssm_a_re, ssm_a_im, ssm_log_dt, ssm_b_re, ssm_b_im, ssm_c_re, ssm_c_im,
              ssm_d, ssm_w_glu, kv_norm_g, w_kv,
              attn_w_q, lam_q1, lam_k1, lam_q2, lam_k2, attn_subln_g, attn_w_o,
              mlp_w_up, mlp_w_down):
    L = x.shape[1]
    cos, sin = rope_tables(L)
    h = x
    k1 = k2 = v = None
    for l in range(DEPTH):
        hn = rms_norm(h, mix_pre_g[l])
        if l < N_A_LAYERS:
            a = l
            mix = s5_mixer(hn, ssm_w_in[a], ssm_a_re[a], ssm_a_im[a], ssm_log_dt[a],
                           ssm_b_re[a], ssm_b_im[a], ssm_c_re[a], ssm_c_im[a], ssm_d[a], ssm_w_glu[a])
        else:
            b = l - N_A_LAYERS
            lambda_init = 0.8 - 0.6 * math.exp(-0.3 * l)
            mix = diff_attention(hn, attn_w_q[b], lam_q1[b], lam_k1[b], lam_q2[b], lam_k2[b],
                                 attn_subln_g[b], attn_w_o[b], k1, k2, v, cos, sin, lambda_init)
        h = h + rms_norm(mix, mix_post_g[l])
        ff = sq_relu_mlp(rms_norm(h, mlp_pre_g[l]), mlp_w_up[l], mlp_w_down[l])
        h = h + rms_norm(ff, mlp_post_g[l])
        if l == N_A_LAYERS - 1:
            k1, k2, v = shared_kv(h, kv_norm_g, w_kv, cos, sin)
    return h
```

```python
import functools
import math

import jax
import jax.numpy as jnp
from jax import lax
from jax.experimental import pallas as pl
from jax.experimental.pallas import tpu as pltpu

EPS = 1e-6
ROPE_THETA = 10000.0
LANES = 128
SSM_TC = 8
VMEM_LIMIT_BYTES = 56 * 1024 * 1024
MASK_VALUE = -1e30
MLP_DOWN_COLS = 512

F32 = jnp.float32
BF16 = jnp.bfloat16


def _params(*sem):
    return pltpu.CompilerParams(dimension_semantics=sem, vmem_limit_bytes=VMEM_LIMIT_BYTES)


def _rms(x, g):
    return x * lax.rsqrt(jnp.mean(x * x, axis=-1, keepdims=True) + EPS) * g


def _dot(a, b):
    return jnp.dot(a, b, preferred_element_type=F32)


def _rope(t, cos, sin_signed):
    return t * cos + pltpu.roll(t, LANES // 2, 1) * sin_signed


def _make_norm_matmul_kernel(rope_blocks, scale, j_axis):
    def kernel(x_ref, g_ref, w_ref, *rest):
        if rope_blocks:
            cos_ref, sin_ref, o_ref, xn_ref = rest
        else:
            o_ref, xn_ref = rest
        j = pl.program_id(j_axis)

        @pl.when(j == 0)
        def _():
            xn_ref[...] = _rms(x_ref[...], g_ref[...]).astype(BF16)

        acc = _dot(xn_ref[...], w_ref[...])
        if not rope_blocks:
            o_ref[...] = acc.astype(o_ref.dtype)
            return

        @pl.when(j < rope_blocks)
        def _():
            cos = cos_ref[...]
            sin = sin_ref[...]
            for c in range(acc.shape[1] // LANES):
                sl = slice(c * LANES, (c + 1) * LANES)
                o_ref[:, sl] = (_rope(acc[:, sl], cos, sin) * scale).astype(o_ref.dtype)

        @pl.when(j >= rope_blocks)
        def _():
            o_ref[...] = acc.astype(o_ref.dtype)

    return kernel


def _norm_matmul(x, g, w, *, out_dtype, bm, bn, rope=None, rope_cols=0, scale=1.0):
    m, k = x.shape
    n = w.shape[1]
    bm, bn = min(bm, m), min(bn, n)
    if rope is not None:
        bn = min(bn, rope_cols)
        assert rope_cols % bn == 0
    in_specs = [
        pl.BlockSpec((bm, k), lambda i, j: (i, 0)),
        pl.BlockSpec((1, k), lambda i, j: (0, 0)),
        pl.BlockSpec((k, bn), lambda i, j: (0, j)),
    ]
    args = [x, g.reshape(1, k), w]
    if rope is not None:
        cos, sin = rope
        seq = cos.shape[0]
        bm = min(bm, seq)
        in_specs[0] = pl.BlockSpec((bm, k), lambda i, j: (i, 0))
        nseq = seq // bm
        in_specs += [pl.BlockSpec((bm, LANES), lambda i, j: (i % nseq, 0))] * 2
        args += [cos, sin]
    return pl.pallas_call(
        _make_norm_matmul_kernel(rope_cols // bn, scale, 1),
        grid=(m // bm, n // bn),
        in_specs=in_specs,
        out_specs=pl.BlockSpec((bm, bn), lambda i, j: (i, j)),
        out_shape=jax.ShapeDtypeStruct((m, n), out_dtype),
        scratch_shapes=[pltpu.VMEM((bm, k), BF16)],
        compiler_params=_params("parallel", "arbitrary"),
    )(*args)


def _ssm_in_proj(x2, g, w, *, tc, bm, bn):
    r, tcd = x2.shape
    d = tcd // tc
    n = w.shape[1]
    bm, bn = min(bm, r), min(bn, n)
    return pl.pallas_call(
        _make_norm_matmul_kernel(0, 1.0, 2),
        grid=(tc, r // bm, n // bn),
        in_specs=[
            pl.BlockSpec((bm, d), lambda s, i, j: (i, s)),
            pl.BlockSpec((1, d), lambda s, i, j: (0, 0)),
            pl.BlockSpec((d, bn), lambda s, i, j: (0, j)),
        ],
        out_specs=pl.BlockSpec((None, bm, bn), lambda s, i, j: (s, i, j)),
        out_shape=jax.ShapeDtypeStruct((tc, r, n), F32),
        scratch_shapes=[pltpu.VMEM((bm, d), BF16)],
        compiler_params=_params("parallel", "parallel", "arbitrary"),
    )(x2, g.reshape(1, d), w)


def _gelu_tanh(x):
    return 0.5 * x * (1.0 + jnp.tanh(math.sqrt(2.0 / math.pi) * (x + 0.044715 * (x * x * x))))


def _ssm_kernel(u_ref, wi_ref, wn_ref, wo_ref, a_ref, d_ref, z_ref, st_ref, *, nb, nc):
    tc = u_ref.shape[0]
    nk = st_ref.shape[0] // 2
    lhs = jnp.concatenate([u_ref[s].astype(BF16) for s in range(tc)], axis=1)
    contrib = _dot(lhs, wn_ref[...])
    for k in range(2 * nk):
        st_ref[k] = contrib[:, k * LANES:(k + 1) * LANES]

    a = a_ref[...]
    ar = [jnp.broadcast_to(a[0:1, k * LANES:(k + 1) * LANES], (nb, LANES)) for k in range(nk)]
    ai = [jnp.broadcast_to(a[1:2, k * LANES:(k + 1) * LANES], (nb, LANES)) for k in range(nk)]

    def step(c, carry):
        rows = pl.ds(c, nb, stride=nc)
        out = []
        for k in range(nk):
            sr, si = carry[2 * k], carry[2 * k + 1]
            cr = st_ref[k, rows, :]
            ci = st_ref[nk + k, rows, :]
            st_ref[k, rows, :] = sr
            st_ref[nk + k, rows, :] = si
            out += [ar[k] * sr - ai[k] * si + cr, ar[k] * si + ai[k] * sr + ci]
        return tuple(out)

    zero = jnp.zeros((nb, LANES), F32)
    lax.fori_loop(0, nc, step, (zero,) * (2 * nk))

    state = jnp.concatenate([st_ref[k].astype(BF16) for k in range(2 * nk)], axis=1)
    y = _dot(lhs, wi_ref[...]) + _dot(state, wo_ref[...])
    d = d_ref[...]
    for t in range(tc):
        yt = y[:, t * LANES:(t + 1) * LANES] + d * u_ref[t]
        z_ref[t] = _gelu_tanh(yt).astype(z_ref.dtype)


def _ssm_weights(a_re, a_im, log_dt, b_re, b_im, c_re, c_im, tc):
    g_n, n_st = a_re.shape
    p_ch = b_re.shape[-1]
    gpb = LANES // p_ch
    nj = g_n // gpb
    hi = lax.Precision.HIGHEST
    step = jnp.exp(log_dt.astype(F32))[:, None]
    lam_re = jnp.minimum(a_re.astype(F32), -1e-4)
    lam_im = a_im.astype(F32)
    mag = jnp.exp(step * lam_re)
    abar_re = mag * jnp.cos(step * lam_im)
    abar_im = mag * jnp.sin(step * lam_im)
    den = lam_re * lam_re + lam_im * lam_im
    nr = abar_re - 1.0
    ni = abar_im
    coef_re = (nr * lam_re + ni * lam_im) / den
    coef_im = (ni * lam_re - nr * lam_im) / den
    bre, bim = b_re.astype(F32), b_im.astype(F32)
    bbar_re = coef_re[..., None] * bre - coef_im[..., None] * bim
    bbar_im = coef_re[..., None] * bim + coef_im[..., None] * bre
    dd = jnp.arange(tc + 1, dtype=F32)[:, None, None]
    pmag = jnp.exp(dd * (step * lam_re))
    pw_re = pmag * jnp.cos(dd * (step * lam_im))
    pw_im = pmag * jnp.sin(dd * (step * lam_im))
    eb_re = pw_re[:tc, :, :, None] * bbar_re - pw_im[:tc, :, :, None] * bbar_im
    eb_im = pw_re[:tc, :, :, None] * bbar_im + pw_im[:tc, :, :, None] * bbar_re
    cre, cim = c_re.astype(F32), c_im.astype(F32)
    kd = (jnp.einsum('gpn,dgnq->dgpq', cre, eb_re, precision=hi)
          - jnp.einsum('gpn,dgnq->dgpq', cim, eb_im, precision=hi))
    eye = jnp.eye(gpb, dtype=F32)

    lag = jnp.arange(tc)[None, :] - jnp.arange(tc)[:, None]
    toep = kd.reshape(tc, nj, gpb, p_ch, p_ch)[jnp.maximum(lag, 0)]
    toep = jnp.where((lag >= 0)[:, :, None, None, None, None], toep, 0.0)
    toep = toep.transpose(2, 0, 3, 5, 1, 4)
    w_intra = toep[:, :, :, :, :, None, :] * eye[None, None, :, None, None, :, None]
    w_intra = w_intra.reshape(nj, tc * LANES, tc * LANES)

    e_in = jnp.stack([eb_re[::-1], eb_im[::-1]], axis=1)
    e_in = e_in.reshape(tc, 2, nj, gpb, n_st, p_ch).transpose(2, 0, 3, 5, 1, 4)
    w_in = e_in[:, :, :, :, :, None, :] * eye[None, None, :, None, None, :, None]
    w_in = w_in.reshape(nj, tc * LANES, 2 * gpb * n_st)

    pr, pi = pw_re[1:, :, None, :], pw_im[1:, :, None, :]
    er = cre[None] * pr - cim[None] * pi
    ei = cre[None] * pi + cim[None] * pr
    e_out = jnp.stack([er, -ei], axis=0)
    e_out = e_out.reshape(2, tc, nj, gpb, p_ch, n_st).transpose(2, 0, 3, 5, 1, 4)
    w_out = e_out[:, :, :, :, :, None, :] * eye[None, None, :, None, None, :, None]
    w_out = w_out.reshape(nj, 2 * gpb * n_st, tc * LANES)

    a_step = jnp.stack([pw_re[tc], pw_im[tc]], axis=0)
    a_step = a_step.reshape(2, nj, gpb * n_st).transpose(1, 0, 2)
    return w_intra.astype(BF16), w_in.astype(BF16), w_out.astype(BF16), a_step


def _ssm_scan(u, w_intra, w_in, w_out, a_step, d_skip, *, batch):
    tc, r, d = u.shape
    nj = d // LANES
    nc = r // batch
    nb = min(batch, 4)
    rb = nb * nc
    s2 = w_in.shape[2]
    return pl.pallas_call(
        functools.partial(_ssm_kernel, nb=nb, nc=nc),
        grid=(nj, r // rb),
        in_specs=[
            pl.BlockSpec((tc, rb, LANES), lambda j, b: (0, b, j)),
            pl.BlockSpec((None, tc * LANES, tc * LANES), lambda j, b: (j, 0, 0)),
            pl.BlockSpec((None, tc * LANES, s2), lambda j, b: (j, 0, 0)),
            pl.BlockSpec((None, s2, tc * LANES), lambda j, b: (j, 0, 0)),
            pl.BlockSpec((None, 2, s2 // 2), lambda j, b: (j, 0, 0)),
            pl.BlockSpec((1, LANES), lambda j, b: (0, j)),
        ],
        out_specs=pl.BlockSpec((tc, rb, LANES), lambda j, b: (0, b, j)),
        out_shape=jax.ShapeDtypeStruct((tc, r, d), BF16),
        scratch_shapes=[pltpu.VMEM((s2 // LANES, rb, LANES), F32)],
        compiler_params=_params("parallel", "parallel"),
    )(u, w_intra, w_in, w_out, a_step, d_skip.reshape(1, d))


def _make_matmul_norm_res_kernel(glu, nj, bn, j_axis):
    def kernel(a_ref, *rest):
        if glu:
            wv_ref, wg_ref, x_ref, g_ref, o_ref = rest
        else:
            wv_ref, x_ref, g_ref, o_ref = rest
        j = pl.program_id(j_axis)
        a = a_ref[...]
        mix = _dot(a, wv_ref[...])
        if glu:
            gate = _dot(a, wg_ref[...])
            mix = mix * (1.0 / (1.0 + jnp.exp(-gate)))
        for jj in range(nj):
            @pl.when(j == jj)
            def _(jj=jj):
                o_ref[:, jj * bn:(jj + 1) * bn] = mix

        @pl.when(j == nj - 1)
        def _():
            o_ref[...] = x_ref[...] + _rms(o_ref[...], g_ref[...])

    return kernel


def _glu_norm_res(z, w_glu, x2, g, *, bm, bn):
    tc, r, d = z.shape
    bm, bn = min(bm, r), min(bn, d)
    nj = d // bn
    return pl.pallas_call(
        _make_matmul_norm_res_kernel(True, nj, bn, 2),
        grid=(tc, r // bm, nj),
        in_specs=[
            pl.BlockSpec((None, bm, d), lambda t, i, j: (t, i, 0)),
            pl.BlockSpec((d, bn), lambda t, i, j: (0, j)),
            pl.BlockSpec((d, bn), lambda t, i, j: (0, j + nj)),
            pl.BlockSpec((bm, d), lambda t, i, j: (i, t)),
            pl.BlockSpec((1, d), lambda t, i, j: (0, 0)),
        ],
        out_specs=pl.BlockSpec((bm, d), lambda t, i, j: (i, t)),
        out_shape=jax.ShapeDtypeStruct(x2.shape, F32),
        compiler_params=_params("parallel", "parallel", "arbitrary"),
    )(z, w_glu, w_glu, x2, g.reshape(1, d))


def _matmul_norm_res(a, w, x, g, *, bm, bn):
    m, k = a.shape
    d = w.shape[1]
    bm, bn = min(bm, m), min(bn, d)
    nj = d // bn
    return pl.pallas_call(
        _make_matmul_norm_res_kernel(False, nj, bn, 1),
        grid=(m // bm, nj),
        in_specs=[
            pl.BlockSpec((bm, k), lambda i, j: (i, 0)),
            pl.BlockSpec((k, bn), lambda i, j: (0, j)),
            pl.BlockSpec((bm, d), lambda i, j: (i, 0)),
            pl.BlockSpec((1, d), lambda i, j: (0, 0)),
        ],
        out_specs=pl.BlockSpec((bm, d), lambda i, j: (i, 0)),
        out_shape=jax.ShapeDtypeStruct((m, d), F32),
        compiler_params=_params("parallel", "arbitrary"),
    )(a, w, x, g.reshape(1, d))


def _mlp_kernel(x_ref, gpre_ref, wu_ref, wd_ref, gpost_ref, o_ref, xn_ref):
    f = pl.program_id(1)

    @pl.when(f == 0)
    def _():
        xn_ref[...] = _rms(x_ref[...], gpre_ref[...]).astype(BF16)
        o_ref[...] = jnp.zeros_like(o_ref)

    h = _dot(xn_ref[...], wu_ref[...])
    a = jnp.square(jnp.maximum(h, 0.0)).astype(BF16)
    bn = min(MLP_DOWN_COLS, o_ref.shape[1])
    for c in range(o_ref.shape[1] // bn):
        sl = slice(c * bn, (c + 1) * bn)
        o_ref[:, sl] += _dot(a, wd_ref[:, sl])

    @pl.when(f == pl.num_programs(1) - 1)
    def _():
        o_ref[...] = x_ref[...] + _rms(o_ref[...], gpost_ref[...])


def _mlp(x, g_pre, w_up, w_down, g_post, *, bm, bf):
    m, d = x.shape
    dff = w_up.shape[1]
    bm, bf = min(bm, m), min(bf, dff)
    return pl.pallas_call(
        _mlp_kernel,
        grid=(m // bm, dff // bf),
        in_specs=[
            pl.BlockSpec((bm, d), lambda i, f: (i, 0), pipeline_mode=pl.Buffered(1)),
            pl.BlockSpec((1, d), lambda i, f: (0, 0)),
            pl.BlockSpec((d, bf), lambda i, f: (0, f)),
            pl.BlockSpec((bf, d), lambda i, f: (f, 0)),
            pl.BlockSpec((1, d), lambda i, f: (0, 0)),
        ],
        out_specs=pl.BlockSpec((bm, d), lambda i, f: (i, 0)),
        out_shape=jax.ShapeDtypeStruct((m, d), F32),
        scratch_shapes=[pltpu.VMEM((bm, d), BF16)],
        compiler_params=_params("parallel", "arbitrary"),
    )(x, g_pre.reshape(1, d), w_up, w_down, g_post.reshape(1, d))


def _attn_kernel(q_ref, k_ref, v_ref, lam_ref, g_ref, o_ref, *, blk, hd, lambda_init):
    qi = pl.program_id(2)
    nt = (((1,), (1,)), ((), ()))

    def softmax_v(q, col):
        def scores(ki):
            k = k_ref[pl.ds(ki * blk, blk), col:col + hd]
            return lax.dot_general(q, k, nt, preferred_element_type=F32)

        def update(carry, s, ki):
            m, l, acc = carry
            m_new = jnp.maximum(m, jnp.max(s, axis=1, keepdims=True))
            p = jnp.exp(s - m_new)
            alpha = jnp.exp(m - m_new)
            l = alpha * l + jnp.sum(p, axis=1, keepdims=True)
            v = v_ref[pl.ds(ki * blk, blk), :]
            acc = alpha * acc + _dot(p.astype(BF16), v)
            return m_new, l, acc

        init = (jnp.full((blk, 1), MASK_VALUE, F32), jnp.zeros((blk, 1), F32),
                jnp.zeros((blk, v_ref.shape[1]), F32))
        carry = lax.fori_loop(0, qi, lambda ki, c: update(c, scores(ki), ki), init)
        row = lax.broadcasted_iota(jnp.int32, (blk, blk), 0)
        colid = lax.broadcasted_iota(jnp.int32, (blk, blk), 1)
        s = jnp.where(colid <= row, scores(qi), MASK_VALUE)
        _, l, acc = update(carry, s, qi)
        return acc / l

    lv = lam_ref[...]
    lam = (jnp.exp(jnp.sum(lv[0:1] * lv[1:2], axis=1, keepdims=True))
           - jnp.exp(jnp.sum(lv[2:3] * lv[3:4], axis=1, keepdims=True)) + lambda_init)
    o = softmax_v(q_ref[:, 0:hd], 0) - lam * softmax_v(q_ref[:, hd:2 * hd], hd)
    o_ref[...] = (_rms(o, g_ref[...]) * (1.0 - lambda_init)).astype(o_ref.dtype)


def _diff_attention(q, kv, lam_vecs, g_sub, *, batch, heads, hd, lambda_init, blk):
    t, d = q.shape
    seq = t // batch
    blk = min(blk, seq)
    nq = seq // blk
    return pl.pallas_call(
        functools.partial(_attn_kernel, blk=blk, hd=hd, lambda_init=lambda_init),
        grid=(batch, heads, nq),
        in_specs=[
            pl.BlockSpec((blk, 2 * hd), lambda b, h, i: (b * nq + i, h)),
            pl.BlockSpec((seq, 2 * hd), lambda b, h, i: (b, h)),
            pl.BlockSpec((seq, 2 * hd), lambda b, h, i: (b, heads + h)),
            pl.BlockSpec((8, hd), lambda b, h, i: (0, 0)),
            pl.BlockSpec((1, 2 * hd), lambda b, h, i: (0, 0)),
        ],
        out_specs=pl.BlockSpec((blk, 2 * hd), lambda b, h, i: (b * nq + i, h)),
        out_shape=jax.ShapeDtypeStruct((t, d), BF16),
        compiler_params=_params("parallel", "parallel", "arbitrary"),
    )(q, kv, kv, lam_vecs, g_sub.reshape(1, 2 * hd))


def _rope_tables(seq, hd):
    pos = jnp.arange(seq, dtype=F32)
    inv_freq = 1.0 / (ROPE_THETA ** (jnp.arange(0, hd, 2, dtype=F32) / hd))
    ang = pos[:, None] * inv_freq[None, :]
    emb = jnp.concatenate([ang, ang], axis=-1)
    sign = jnp.where(jnp.arange(hd) < hd // 2, -1.0, 1.0).astype(F32)
    return jnp.cos(emb), jnp.sin(emb) * sign


def kernel(x, mix_pre_g, mix_post_g, mlp_pre_g, mlp_post_g, ssm_w_in, ssm_a_re, ssm_a_im, ssm_log_dt, ssm_b_re, ssm_b_im, ssm_c_re, ssm_c_im, ssm_d, ssm_w_glu, kv_norm_g, w_kv, attn_w_q, lam_q1, lam_k1, lam_q2, lam_k2, attn_subln_g, attn_w_o, mlp_w_up, mlp_w_down):
    batch, seq, d = x.shape
    t = batch * seq
    depth = mix_pre_g.shape[0]
    n_a = ssm_w_in.shape[0]
    hd = lam_q1.shape[1]
    heads = d // (2 * hd)
    tc = SSM_TC
    r = t // tc
    cos, sin = _rope_tables(seq, hd)

    h = x.reshape(t, d)
    kv = None
    for l in range(depth):
        if l < n_a:
            a = l
            u = _ssm_in_proj(h.reshape(r, tc * d), mix_pre_g[l], ssm_w_in[a].astype(BF16),
                             tc=tc, bm=1024, bn=1024)
            sw = _ssm_weights(ssm_a_re[a], ssm_a_im[a], ssm_log_dt[a], ssm_b_re[a], ssm_b_im[a],
                              ssm_c_re[a], ssm_c_im[a], tc)
            z = _ssm_scan(u, *sw, ssm_d[a], batch=batch)
            h = _glu_norm_res(z, ssm_w_glu[a].astype(BF16), h.reshape(r, tc * d), mix_post_g[l],
                              bm=512, bn=1024).reshape(t, d)
        else:
            b = l - n_a
            lambda_init = 0.8 - 0.6 * math.exp(-0.3 * l)
            q = _norm_matmul(h, mix_pre_g[l], attn_w_q[b].astype(BF16), out_dtype=BF16, bm=1024, bn=1024,
                             rope=(cos, sin), rope_cols=d, scale=hd ** -0.5)
            lam_vecs = jnp.zeros((8, hd), F32).at[0:4].set(
                jnp.stack([lam_q1[b], lam_k1[b], lam_q2[b], lam_k2[b]]).astype(F32))
            o = _diff_attention(q, kv, lam_vecs, attn_subln_g[b], batch=batch, heads=heads, hd=hd,
                                lambda_init=lambda_init, blk=512)
            h = _matmul_norm_res(o, attn_w_o[b].astype(BF16), h, mix_post_g[l], bm=512, bn=1024)
        h = _mlp(h, mlp_pre_g[l], mlp_w_up[l].astype(BF16), mlp_w_down[l].astype(BF16), mlp_post_g[l],
                 bm=1024, bf=512)
        if l == n_a - 1:
            kv = _norm_matmul(h, kv_norm_g, w_kv.astype(BF16), out_dtype=BF16, bm=1024, bn=1024,
                              rope=(cos, sin), rope_cols=d, scale=1.0)
    return h.reshape(batch, seq, d)
```

```python
import functools
import math

import jax
import jax.numpy as jnp
from jax import lax
from jax.experimental import pallas as pl
from jax.experimental.pallas import tpu as pltpu

EPS = 1e-6
ROPE_THETA = 10000.0
LANES = 128
SSM_TC = 8
SSM_SEQS = 4
VMEM_LIMIT_BYTES = 56 * 1024 * 1024
MASK_VALUE = -1e30
MLP_DOWN_COLS = 512

F32 = jnp.float32
BF16 = jnp.bfloat16


def _params(*sem):
    return pltpu.CompilerParams(dimension_semantics=sem, vmem_limit_bytes=VMEM_LIMIT_BYTES)


def _rms(x, g):
    return x * lax.rsqrt(jnp.mean(x * x, axis=-1, keepdims=True) + EPS) * g


def _dot(a, b):
    return jnp.dot(a, b, preferred_element_type=F32)


def _rope(t, cos, sin_signed):
    return t * cos + pltpu.roll(t, LANES // 2, 1) * sin_signed


def _make_norm_matmul_kernel(rope_blocks, scale):
    def kernel(x_ref, g_ref, w_ref, *rest):
        if rope_blocks:
            cos_ref, sin_ref, o_ref, xn_ref = rest
        else:
            o_ref, xn_ref = rest
        j = pl.program_id(1)

        @pl.when(j == 0)
        def _():
            xn_ref[...] = _rms(x_ref[...], g_ref[...]).astype(BF16)

        acc = _dot(xn_ref[...], w_ref[...])
        if not rope_blocks:
            o_ref[...] = acc.astype(o_ref.dtype)
            return

        @pl.when(j < rope_blocks)
        def _():
            cos = cos_ref[...]
            sin = sin_ref[...]
            for c in range(acc.shape[1] // LANES):
                sl = slice(c * LANES, (c + 1) * LANES)
                o_ref[:, sl] = (_rope(acc[:, sl], cos, sin) * scale).astype(o_ref.dtype)

        @pl.when(j >= rope_blocks)
        def _():
            o_ref[...] = acc.astype(o_ref.dtype)

    return kernel


def _norm_matmul(x, g, w, *, out_dtype, bm, bn, rope=None, rope_cols=0, scale=1.0):
    m, k = x.shape
    n = w.shape[1]
    bm, bn = min(bm, m), min(bn, n)
    args = [x, g.reshape(1, k), w]
    if rope is not None:
        cos, sin = rope
        seq = cos.shape[0]
        bm, bn = min(bm, seq), min(bn, rope_cols)
        assert rope_cols % bn == 0 and seq % bm == 0
        nseq = seq // bm
        args += [cos, sin]
    in_specs = [
        pl.BlockSpec((bm, k), lambda i, j: (i, 0)),
        pl.BlockSpec((1, k), lambda i, j: (0, 0)),
        pl.BlockSpec((k, bn), lambda i, j: (0, j)),
    ]
    if rope is not None:
        in_specs += [pl.BlockSpec((bm, LANES), lambda i, j: (i % nseq, 0))] * 2
    return pl.pallas_call(
        _make_norm_matmul_kernel(rope_cols // bn, scale),
        grid=(m // bm, n // bn),
        in_specs=in_specs,
        out_specs=pl.BlockSpec((bm, bn), lambda i, j: (i, j)),
        out_shape=jax.ShapeDtypeStruct((m, n), out_dtype),
        scratch_shapes=[pltpu.VMEM((bm, k), BF16)],
        compiler_params=_params("parallel", "arbitrary"),
    )(*args)


def _ssm_prep_kernel(pw_ref, bb_ref, cc_ref, wi_ref, wn_ref, wo_ref, *, tc, p_ch, n_st):
    s_dim = pw_ref.shape[1]
    gpb = LANES // p_ch
    row_g = lax.shift_right_logical(lax.broadcasted_iota(jnp.int32, (LANES, s_dim), 0), int(math.log2(p_ch)))
    col_g = lax.shift_right_logical(lax.broadcasted_iota(jnp.int32, (LANES, s_dim), 1), int(math.log2(n_st)))
    same_group = row_g == col_g

    def expand(x):
        return jnp.where(same_group, jnp.concatenate([x] * gpb, axis=0), 0.0)

    b_r, b_i = expand(bb_ref[0]), expand(bb_ref[1])
    c_r, c_i = expand(cc_ref[0]), expand(cc_ref[1])
    c0 = jnp.concatenate([c_r, -c_i], axis=1).T
    kd = []
    for d in range(tc):
        pr, pi = pw_ref[d:d + 1, :], pw_ref[tc + 1 + d:tc + 2 + d, :]
        e = jnp.concatenate([pr * b_r - pi * b_i, pr * b_i + pi * b_r], axis=1)
        wn_ref[(tc - 1 - d) * LANES:(tc - d) * LANES, :] = e.astype(BF16)
        kd.append(jnp.dot(e, c0, precision=lax.Precision.HIGHEST,
                          preferred_element_type=F32).astype(BF16))
    for t in range(tc):
        pr, pi = pw_ref[t + 1:t + 2, :], pw_ref[tc + 2 + t:tc + 3 + t, :]
        e = jnp.concatenate([pr * c_r - pi * c_i, -(pr * c_i + pi * c_r)], axis=1)
        wo_ref[:, t * LANES:(t + 1) * LANES] = e.T.astype(BF16)
    zero = jnp.zeros((LANES, LANES), BF16)
    for s in range(tc):
        for t in range(tc):
            wi_ref[s * LANES:(s + 1) * LANES, t * LANES:(t + 1) * LANES] = kd[t - s] if t >= s else zero


def _ssm_weights(a_re, a_im, log_dt, b_re, b_im, c_re, c_im, tc):
    g_n, n_st = a_re.shape
    p_ch = b_re.shape[-1]
    gpb = LANES // p_ch
    nj = g_n // gpb
    s_dim = gpb * n_st
    assert p_ch & (p_ch - 1) == 0 and n_st & (n_st - 1) == 0
    step = jnp.exp(log_dt.astype(F32))[:, None]
    lam_re = jnp.minimum(a_re.astype(F32), -1e-4)
    lam_im = a_im.astype(F32)
    mag = jnp.exp(step * lam_re)
    abar_re = mag * jnp.cos(step * lam_im)
    abar_im = mag * jnp.sin(step * lam_im)
    den = lam_re * lam_re + lam_im * lam_im
    nr = abar_re - 1.0
    ni = abar_im
    coef_re = (nr * lam_re + ni * lam_im) / den
    coef_im = (ni * lam_re - nr * lam_im) / den
    bre, bim = b_re.astype(F32), b_im.astype(F32)
    bbar_re = coef_re[..., None] * bre - coef_im[..., None] * bim
    bbar_im = coef_re[..., None] * bim + coef_im[..., None] * bre
    dd = jnp.arange(tc + 1, dtype=F32)[:, None, None]
    pmag = jnp.exp(dd * (step * lam_re))
    pw = jnp.concatenate([pmag * jnp.cos(dd * (step * lam_im)),
                          pmag * jnp.sin(dd * (step * lam_im))], axis=0)
    pw = pw.reshape(2 * (tc + 1), nj, s_dim).transpose(1, 0, 2)
    bb = jnp.stack([bbar_re, bbar_im]).reshape(2, nj, gpb, n_st, p_ch)
    bb = bb.transpose(1, 0, 4, 2, 3).reshape(nj, 2, p_ch, s_dim)
    cc = jnp.stack([c_re.astype(F32), c_im.astype(F32)]).reshape(2, nj, gpb, p_ch, n_st)
    cc = cc.transpose(1, 0, 3, 2, 4).reshape(nj, 2, p_ch, s_dim)
    k_dim = tc * LANES
    wshape = jax.ShapeDtypeStruct((nj, k_dim, k_dim), BF16)
    assert 2 * s_dim == k_dim
    w_intra, w_in, w_out = pl.pallas_call(
        functools.partial(_ssm_prep_kernel, tc=tc, p_ch=p_ch, n_st=n_st),
        grid=(nj,),
        in_specs=[
            pl.BlockSpec((None, 2 * (tc + 1), s_dim), lambda j: (j, 0, 0)),
            pl.BlockSpec((None, 2, p_ch, s_dim), lambda j: (j, 0, 0, 0)),
            pl.BlockSpec((None, 2, p_ch, s_dim), lambda j: (j, 0, 0, 0)),
        ],
        out_specs=[pl.BlockSpec((None, k_dim, k_dim), lambda j: (j, 0, 0))] * 3,
        out_shape=[wshape] * 3,
        compiler_params=_params("parallel"),
    )(pw, bb, cc)
    a_step = jnp.stack([pw[:, tc], pw[:, 2 * tc + 1]], axis=1)
    return w_intra, w_in, w_out, a_step


def _gelu_tanh(x):
    return 0.5 * x * (1.0 + jnp.tanh(math.sqrt(2.0 / math.pi) * (x + 0.044715 * (x * x * x))))


def _ssm_kernel(u_ref, wi_ref, wn_ref, wo_ref, a_ref, d_ref, z_ref, st_ref, zs_ref, *, nb, nc):
    tc = SSM_TC
    rows = nb * nc
    nk = st_ref.shape[0] // 2

    def step_rows(s):
        return pl.ds(s, rows, stride=tc)

    lhs = jnp.concatenate([u_ref[step_rows(s), :].astype(BF16) for s in range(tc)], axis=1)
    contrib = _dot(lhs, wn_ref[...])
    for k in range(2 * nk):
        for b in range(nb):
            st_ref[k, pl.ds(b, nc, stride=nb), :] = contrib[b * nc:(b + 1) * nc, k * LANES:(k + 1) * LANES]

    a = a_ref[...]
    ar = [jnp.broadcast_to(a[0:1, k * LANES:(k + 1) * LANES], (nb, LANES)) for k in range(nk)]
    ai = [jnp.broadcast_to(a[1:2, k * LANES:(k + 1) * LANES], (nb, LANES)) for k in range(nk)]

    def step(c, carry):
        rws = pl.ds(pl.multiple_of(c * nb, nb), nb)
        out = []
        for k in range(nk):
            sr, si = carry[2 * k], carry[2 * k + 1]
            cr = st_ref[k, rws, :]
            ci = st_ref[nk + k, rws, :]
            st_ref[k, rws, :] = sr
            st_ref[nk + k, rws, :] = si
            out += [ar[k] * sr - ai[k] * si + cr, ar[k] * si + ai[k] * sr + ci]
        return tuple(out)

    zero = jnp.zeros((nb, LANES), F32)
    lax.fori_loop(0, nc, step, (zero,) * (2 * nk))

    state = jnp.concatenate(
        [jnp.concatenate([st_ref[k, pl.ds(b, nc, stride=nb), :] for b in range(nb)], axis=0).astype(BF16)
         for k in range(2 * nk)], axis=1)
    y = _dot(lhs, wi_ref[...]) + _dot(state, wo_ref[...])
    d = d_ref[...]
    for t in range(tc):
        yt = y[:, t * LANES:(t + 1) * LANES] + d * u_ref[step_rows(t), :]
        zs_ref[step_rows(t), :] = _gelu_tanh(yt)
    z_ref[...] = zs_ref[...].astype(z_ref.dtype)


def _ssm_scan(u, w_intra, w_in, w_out, a_step, d_skip, *, batch):
    t, d = u.shape
    seq = t // batch
    tc = SSM_TC
    nc = seq // tc
    nb = min(batch, SSM_SEQS)
    k_dim = tc * LANES
    return pl.pallas_call(
        functools.partial(_ssm_kernel, nb=nb, nc=nc),
        grid=(d // LANES, batch // nb),
        in_specs=[
            pl.BlockSpec((nb * seq, LANES), lambda j, b: (b, j)),
            pl.BlockSpec((None, k_dim, k_dim), lambda j, b: (j, 0, 0)),
            pl.BlockSpec((None, k_dim, k_dim), lambda j, b: (j, 0, 0)),
            pl.BlockSpec((None, k_dim, k_dim), lambda j, b: (j, 0, 0)),
            pl.BlockSpec((None, 2, k_dim // 2), lambda j, b: (j, 0, 0)),
            pl.BlockSpec((1, LANES), lambda j, b: (0, j)),
        ],
        out_specs=pl.BlockSpec((nb * seq, LANES), lambda j, b: (b, j)),
        out_shape=jax.ShapeDtypeStruct((t, d), BF16),
        scratch_shapes=[pltpu.VMEM((k_dim // LANES, nc * nb, LANES), F32),
                        pltpu.VMEM((nb * seq, LANES), F32)],
        compiler_params=_params("parallel", "parallel"),
    )(u, w_intra, w_in, w_out, a_step, d_skip.reshape(1, d))


def _make_matmul_norm_res_kernel(glu, nj, bn):
    def kernel(a_ref, *rest):
        if glu:
            wv_ref, wg_ref, x_ref, g_ref, o_ref = rest
        else:
            wv_ref, x_ref, g_ref, o_ref = rest
        j = pl.program_id(1)
        a = a_ref[...]
        mix = _dot(a, wv_ref[...])
        if glu:
            gate = _dot(a, wg_ref[...])
            mix = mix * (1.0 / (1.0 + jnp.exp(-gate)))
        for jj in range(nj):
            @pl.when(j == jj)
            def _(jj=jj):
                o_ref[:, jj * bn:(jj + 1) * bn] = mix

        @pl.when(j == nj - 1)
        def _():
            o_ref[...] = x_ref[...] + _rms(o_ref[...], g_ref[...])

    return kernel


def _matmul_norm_res(a, w, x, g, *, bm, bn, glu=False):
    m, k = a.shape
    d = x.shape[1]
    bm, bn = min(bm, m), min(bn, d)
    nj = d // bn
    w_specs = [pl.BlockSpec((k, bn), lambda i, j: (0, j))]
    if glu:
        w_specs.append(pl.BlockSpec((k, bn), lambda i, j: (0, j + nj)))
    return pl.pallas_call(
        _make_matmul_norm_res_kernel(glu, nj, bn),
        grid=(m // bm, nj),
        in_specs=[pl.BlockSpec((bm, k), lambda i, j: (i, 0))] + w_specs + [
            pl.BlockSpec((bm, d), lambda i, j: (i, 0)),
            pl.BlockSpec((1, d), lambda i, j: (0, 0)),
        ],
        out_specs=pl.BlockSpec((bm, d), lambda i, j: (i, 0)),
        out_shape=jax.ShapeDtypeStruct((m, d), F32),
        compiler_params=_params("parallel", "arbitrary"),
    )(a, *([w] * len(w_specs)), x, g.reshape(1, d))


def _mlp_kernel(x_ref, gpre_ref, wu_ref, wd_ref, gpost_ref, o_ref, xn_ref):
    f = pl.program_id(1)

    @pl.when(f == 0)
    def _():
        xn_ref[...] = _rms(x_ref[...], gpre_ref[...]).astype(BF16)
        o_ref[...] = jnp.zeros_like(o_ref)

    h = _dot(xn_ref[...], wu_ref[...])
    a = jnp.square(jnp.maximum(h, 0.0)).astype(BF16)
    bn = min(MLP_DOWN_COLS, o_ref.shape[1])
    for c in range(o_ref.shape[1] // bn):
        sl = slice(c * bn, (c + 1) * bn)
        o_ref[:, sl] += _dot(a, wd_ref[:, sl])

    @pl.when(f == pl.num_programs(1) - 1)
    def _():
        o_ref[...] = x_ref[...] + _rms(o_ref[...], gpost_ref[...])


def _mlp(x, g_pre, w_up, w_down, g_post, *, bm, bf):
    m, d = x.shape
    dff = w_up.shape[1]
    bm, bf = min(bm, m), min(bf, dff)
    return pl.pallas_call(
        _mlp_kernel,
        grid=(m // bm, dff // bf),
        in_specs=[
            pl.BlockSpec((bm, d), lambda i, f: (i, 0), pipeline_mode=pl.Buffered(1)),
            pl.BlockSpec((1, d), lambda i, f: (0, 0)),
            pl.BlockSpec((d, bf), lambda i, f: (0, f)),
            pl.BlockSpec((bf, d), lambda i, f: (f, 0)),
            pl.BlockSpec((1, d), lambda i, f: (0, 0)),
        ],
        out_specs=pl.BlockSpec((bm, d), lambda i, f: (i, 0)),
        out_shape=jax.ShapeDtypeStruct((m, d), F32),
        scratch_shapes=[pltpu.VMEM((bm, d), BF16)],
        compiler_params=_params("parallel", "arbitrary"),
    )(x, g_pre.reshape(1, d), w_up, w_down, g_post.reshape(1, d))


def _attn_kernel(q_ref, k_ref, v_ref, lam_ref, g_ref, o_ref, *, blk, hd, nq, lambda_init):
    qi = pl.program_id(2)
    nt = (((1,), (1,)), ((), ()))
    lv = lam_ref[...]
    lam = (jnp.exp(jnp.sum(lv[0:1] * lv[1:2], axis=1, keepdims=True))
           - jnp.exp(jnp.sum(lv[2:3] * lv[3:4], axis=1, keepdims=True)) + lambda_init)
    row = lax.broadcasted_iota(jnp.int32, (blk, blk), 0)
    col = lax.broadcasted_iota(jnp.int32, (blk, blk), 1)
    causal = col <= row

    def softmax_v(c0, kv_len):
        s = lax.dot_general(q_ref[:, c0:c0 + hd], k_ref[0:kv_len, c0:c0 + hd], nt,
                            preferred_element_type=F32)
        s_diag = jnp.where(causal, s[:, kv_len - blk:], MASK_VALUE)
        s = s_diag if kv_len == blk else jnp.concatenate([s[:, :kv_len - blk], s_diag], axis=1)
        p = jnp.exp(s - jnp.max(s, axis=1, keepdims=True))
        l = jnp.sum(p, axis=1, keepdims=True)
        return _dot(p.astype(BF16), v_ref[0:kv_len, :]) * (1.0 / l)

    for n in range(1, nq + 1):
        @pl.when(qi == n - 1)
        def _(n=n):
            o = softmax_v(0, n * blk) - lam * softmax_v(hd, n * blk)
            o_ref[...] = (_rms(o, g_ref[...]) * (1.0 - lambda_init)).astype(o_ref.dtype)


def _diff_attention(q, kv, lam_vecs, g_sub, *, batch, heads, hd, lambda_init, blk):
    t, d = q.shape
    seq = t // batch
    blk = min(blk, seq)
    nq = seq // blk
    return pl.pallas_call(
        functools.partial(_attn_kernel, blk=blk, hd=hd, nq=nq, lambda_init=lambda_init),
        grid=(batch, heads, nq),
        in_specs=[
            pl.BlockSpec((blk, 2 * hd), lambda b, h, i: (b * nq + i, h)),
            pl.BlockSpec((seq, 2 * hd), lambda b, h, i: (b, h)),
            pl.BlockSpec((seq, 2 * hd), lambda b, h, i: (b, heads + h)),
            pl.BlockSpec((8, hd), lambda b, h, i: (0, 0)),
            pl.BlockSpec((1, 2 * hd), lambda b, h, i: (0, 0)),
        ],
        out_specs=pl.BlockSpec((blk, 2 * hd), lambda b, h, i: (b * nq + i, h)),
        out_shape=jax.ShapeDtypeStruct((t, d), BF16),
        compiler_params=_params("parallel", "parallel", "arbitrary"),
    )(q, kv, kv, lam_vecs, g_sub.reshape(1, 2 * hd))


def _rope_tables(seq, hd):
    pos = jnp.arange(seq, dtype=F32)
    inv_freq = 1.0 / (ROPE_THETA ** (jnp.arange(0, hd, 2, dtype=F32) / hd))
    ang = pos[:, None] * inv_freq[None, :]
    emb = jnp.concatenate([ang, ang], axis=-1)
    sign = jnp.where(jnp.arange(hd) < hd // 2, -1.0, 1.0).astype(F32)
    return jnp.cos(emb), jnp.sin(emb) * sign


def kernel(x, mix_pre_g, mix_post_g, mlp_pre_g, mlp_post_g, ssm_w_in, ssm_a_re, ssm_a_im, ssm_log_dt, ssm_b_re, ssm_b_im, ssm_c_re, ssm_c_im, ssm_d, ssm_w_glu, kv_norm_g, w_kv, attn_w_q, lam_q1, lam_k1, lam_q2, lam_k2, attn_subln_g, attn_w_o, mlp_w_up, mlp_w_down):
    batch, seq, d = x.shape
    t = batch * seq
    depth = mix_pre_g.shape[0]
    n_a = ssm_w_in.shape[0]
    hd = lam_q1.shape[1]
    heads = d // (2 * hd)
    cos, sin = _rope_tables(seq, hd)

    h = x.reshape(t, d)
    kv = None
    for l in range(depth):
        if l < n_a:
            a = l
            u = _norm_matmul(h, mix_pre_g[l], ssm_w_in[a].astype(BF16), out_dtype=F32, bm=1024, bn=1024)
            sw = _ssm_weights(ssm_a_re[a], ssm_a_im[a], ssm_log_dt[a], ssm_b_re[a], ssm_b_im[a],
                              ssm_c_re[a], ssm_c_im[a], SSM_TC)
            z = _ssm_scan(u, *sw, ssm_d[a], batch=batch)
            h = _matmul_norm_res(z, ssm_w_glu[a].astype(BF16), h, mix_post_g[l], bm=512, bn=1024, glu=True)
        else:
            b = l - n_a
            lambda_init = 0.8 - 0.6 * math.exp(-0.3 * l)
            q = _norm_matmul(h, mix_pre_g[l], attn_w_q[b].astype(BF16), out_dtype=BF16, bm=1024, bn=1024,
                             rope=(cos, sin), rope_cols=d, scale=hd ** -0.5)
            lam_vecs = jnp.zeros((8, hd), F32).at[0:4].set(
                jnp.stack([lam_q1[b], lam_k1[b], lam_q2[b], lam_k2[b]]).astype(F32))
            o = _diff_attention(q, kv, lam_vecs, attn_subln_g[b], batch=batch, heads=heads, hd=hd,
                                lambda_init=lambda_init, blk=512)
            h = _matmul_norm_res(o, attn_w_o[b].astype(BF16), h, mix_post_g[l], bm=512, bn=1024)
        h = _mlp(h, mlp_pre_g[l], mlp_w_up[l].astype(BF16), mlp_w_down[l].astype(BF16), mlp_post_g[l],
                 bm=1024, bf=512)
        if l == n_a - 1:
            kv = _norm_matmul(h, kv_norm_g, w_kv.astype(BF16), out_dtype=BF16, bm=1024, bn=1024,
                              rope=(cos, sin), rope_cols=d, scale=1.0)
    return h.reshape(batch, seq, d)
```

```python
import functools
import math

import jax
import jax.numpy as jnp
from jax import lax
from jax.experimental import pallas as pl
from jax.experimental.pallas import tpu as pltpu

EPS = 1e-6
ROPE_THETA = 10000.0
LANES = 128
SSM_TC = 8
SSM_SEQS = 4
VMEM_LIMIT_BYTES = 56 * 1024 * 1024
MASK_VALUE = -1e30
MLP_DOWN_COLS = 512
ROW_SUB = 256
COL_SLAB = 512

F32 = jnp.float32
BF16 = jnp.bfloat16


def _params(*sem):
    return pltpu.CompilerParams(dimension_semantics=sem, vmem_limit_bytes=VMEM_LIMIT_BYTES)


def _rms(x, g):
    return x * lax.rsqrt(jnp.mean(x * x, axis=-1, keepdims=True) + EPS) * g


def _dot(a, b):
    return jnp.dot(a, b, preferred_element_type=F32)


def _rope(t, cos, sin_signed):
    return t * cos + pltpu.roll(t, LANES // 2, 1) * sin_signed


def _make_norm_matmul_kernel(rope_cols, scale):
    def kernel(x_ref, g_ref, w_ref, *rest):
        if rope_cols:
            cos_ref, sin_ref, o_ref = rest
        else:
            (o_ref,) = rest
        bm, n = o_ref.shape
        sub, slab = min(ROW_SUB, bm), min(COL_SLAB, n)
        g = g_ref[...]
        for r in range(bm // sub):
            rs = slice(r * sub, (r + 1) * sub)
            xn = _rms(x_ref[rs, :], g).astype(BF16)
            for c in range(n // slab):
                acc = _dot(xn, w_ref[:, c * slab:(c + 1) * slab])
                if c * slab < rope_cols:
                    cos, sin = cos_ref[rs, :], sin_ref[rs, :]
                    for cc in range(slab // LANES):
                        sl = slice(cc * LANES, (cc + 1) * LANES)
                        osl = slice(c * slab + cc * LANES, c * slab + (cc + 1) * LANES)
                        o_ref[rs, osl] = (_rope(acc[:, sl], cos, sin) * scale).astype(o_ref.dtype)
                else:
                    o_ref[rs, c * slab:(c + 1) * slab] = acc.astype(o_ref.dtype)

    return kernel


def _norm_matmul(x, g, w, *, out_dtype, bm, name, rope=None, rope_cols=0, scale=1.0):
    m, k = x.shape
    n = w.shape[1]
    bm = min(bm, m)
    args = [x, g.reshape(1, k), w]
    if rope is not None:
        cos, sin = rope
        seq = cos.shape[0]
        bm = min(bm, seq)
        assert rope_cols % min(COL_SLAB, n) == 0 and seq % bm == 0
        nseq = seq // bm
        args += [cos, sin]
    in_specs = [
        pl.BlockSpec((bm, k), lambda i: (i, 0)),
        pl.BlockSpec((1, k), lambda i: (0, 0)),
        pl.BlockSpec((k, n), lambda i: (0, 0), pipeline_mode=pl.Buffered(1)),
    ]
    if rope is not None:
        in_specs += [pl.BlockSpec((bm, LANES), lambda i: (i % nseq, 0))] * 2
    return pl.pallas_call(
        _make_norm_matmul_kernel(rope_cols, scale),
        grid=(m // bm,),
        in_specs=in_specs,
        out_specs=pl.BlockSpec((bm, n), lambda i: (i, 0)),
        out_shape=jax.ShapeDtypeStruct((m, n), out_dtype),
        compiler_params=_params("parallel"),
        name=name,
    )(*args)


def _ssm_prep_kernel(pw_ref, bb_ref, cc_ref, wi_ref, wn_ref, wo_ref, *, tc, p_ch, n_st):
    s_dim = pw_ref.shape[1]
    gpb = LANES // p_ch
    row_g = lax.shift_right_logical(lax.broadcasted_iota(jnp.int32, (LANES, s_dim), 0), int(math.log2(p_ch)))
    col_g = lax.shift_right_logical(lax.broadcasted_iota(jnp.int32, (LANES, s_dim), 1), int(math.log2(n_st)))
    same_group = row_g == col_g

    def expand(x):
        return jnp.where(same_group, jnp.concatenate([x] * gpb, axis=0), 0.0)

    b_r, b_i = expand(bb_ref[0]), expand(bb_ref[1])
    c_r, c_i = expand(cc_ref[0]), expand(cc_ref[1])
    c0 = jnp.concatenate([c_r, -c_i], axis=1).T
    kd = []
    for d in range(tc):
        pr, pi = pw_ref[d:d + 1, :], pw_ref[tc + 1 + d:tc + 2 + d, :]
        e = jnp.concatenate([pr * b_r - pi * b_i, pr * b_i + pi * b_r], axis=1)
        wn_ref[(tc - 1 - d) * LANES:(tc - d) * LANES, :] = e.astype(BF16)
        kd.append(jnp.dot(e, c0, precision=lax.Precision.HIGHEST,
                          preferred_element_type=F32).astype(BF16))
    for t in range(tc):
        pr, pi = pw_ref[t + 1:t + 2, :], pw_ref[tc + 2 + t:tc + 3 + t, :]
        e = jnp.concatenate([pr * c_r - pi * c_i, -(pr * c_i + pi * c_r)], axis=1)
        wo_ref[:, t * LANES:(t + 1) * LANES] = e.T.astype(BF16)
    zero = jnp.zeros((LANES, LANES), BF16)
    for s in range(tc):
        for t in range(tc):
            wi_ref[s * LANES:(s + 1) * LANES, t * LANES:(t + 1) * LANES] = kd[t - s] if t >= s else zero


def _ssm_weights(a_re, a_im, log_dt, b_re, b_im, c_re, c_im, tc):
    g_n, n_st = a_re.shape
    p_ch = b_re.shape[-1]
    gpb = LANES // p_ch
    nj = g_n // gpb
    s_dim = gpb * n_st
    assert p_ch & (p_ch - 1) == 0 and n_st & (n_st - 1) == 0
    step = jnp.exp(log_dt.astype(F32))[:, None]
    lam_re = jnp.minimum(a_re.astype(F32), -1e-4)
    lam_im = a_im.astype(F32)
    mag = jnp.exp(step * lam_re)
    abar_re = mag * jnp.cos(step * lam_im)
    abar_im = mag * jnp.sin(step * lam_im)
    den = lam_re * lam_re + lam_im * lam_im
    nr = abar_re - 1.0
    ni = abar_im
    coef_re = (nr * lam_re + ni * lam_im) / den
    coef_im = (ni * lam_re - nr * lam_im) / den
    bre, bim = b_re.astype(F32), b_im.astype(F32)
    bbar_re = coef_re[..., None] * bre - coef_im[..., None] * bim
    bbar_im = coef_re[..., None] * bim + coef_im[..., None] * bre
    dd = jnp.arange(tc + 1, dtype=F32)[:, None, None]
    pmag = jnp.exp(dd * (step * lam_re))
    pw = jnp.concatenate([pmag * jnp.cos(dd * (step * lam_im)),
                          pmag * jnp.sin(dd * (step * lam_im))], axis=0)
    pw = pw.reshape(2 * (tc + 1), nj, s_dim).transpose(1, 0, 2)
    bb = jnp.stack([bbar_re, bbar_im]).reshape(2, nj, gpb, n_st, p_ch)
    bb = bb.transpose(1, 0, 4, 2, 3).reshape(nj, 2, p_ch, s_dim)
    cc = jnp.stack([c_re.astype(F32), c_im.astype(F32)]).reshape(2, nj, gpb, p_ch, n_st)
    cc = cc.transpose(1, 0, 3, 2, 4).reshape(nj, 2, p_ch, s_dim)
    k_dim = tc * LANES
    wshape = jax.ShapeDtypeStruct((nj, k_dim, k_dim), BF16)
    assert 2 * s_dim == k_dim
    w_intra, w_in, w_out = pl.pallas_call(
        functools.partial(_ssm_prep_kernel, tc=tc, p_ch=p_ch, n_st=n_st),
        grid=(nj,),
        in_specs=[
            pl.BlockSpec((None, 2 * (tc + 1), s_dim), lambda j: (j, 0, 0)),
            pl.BlockSpec((None, 2, p_ch, s_dim), lambda j: (j, 0, 0, 0)),
            pl.BlockSpec((None, 2, p_ch, s_dim), lambda j: (j, 0, 0, 0)),
        ],
        out_specs=[pl.BlockSpec((None, k_dim, k_dim), lambda j: (j, 0, 0))] * 3,
        out_shape=[wshape] * 3,
        compiler_params=_params("parallel"),
        name="ssm_prep",
    )(pw, bb, cc)
    a_step = jnp.stack([pw[:, tc], pw[:, 2 * tc + 1]], axis=1)
    return w_intra, w_in, w_out, a_step


def _gelu_tanh(x):
    return 0.5 * x * (1.0 + jnp.tanh(math.sqrt(2.0 / math.pi) * (x + 0.044715 * (x * x * x))))


def _ssm_kernel(u_ref, wi_ref, wn_ref, wo_ref, a_ref, d_ref, z_ref, st_ref, zs_ref, *, nb, nc):
    tc = SSM_TC
    rows = nb * nc
    nk = st_ref.shape[0] // 2

    def step_rows(s):
        return pl.ds(s, rows, stride=tc)

    lhs = jnp.concatenate([u_ref[step_rows(s), :].astype(BF16) for s in range(tc)], axis=1)
    contrib = _dot(lhs, wn_ref[...])
    for k in range(2 * nk):
        for b in range(nb):
            st_ref[k, pl.ds(b, nc, stride=nb), :] = contrib[b * nc:(b + 1) * nc, k * LANES:(k + 1) * LANES]

    a = a_ref[...]
    ar = [jnp.broadcast_to(a[0:1, k * LANES:(k + 1) * LANES], (nb, LANES)) for k in range(nk)]
    ai = [jnp.broadcast_to(a[1:2, k * LANES:(k + 1) * LANES], (nb, LANES)) for k in range(nk)]

    def step(c, carry):
        rws = pl.ds(pl.multiple_of(c * nb, nb), nb)
        out = []
        for k in range(nk):
            sr, si = carry[2 * k], carry[2 * k + 1]
            cr = st_ref[k, rws, :]
            ci = st_ref[nk + k, rws, :]
            st_ref[k, rws, :] = sr
            st_ref[nk + k, rws, :] = si
            out += [ar[k] * sr - ai[k] * si + cr, ar[k] * si + ai[k] * sr + ci]
        return tuple(out)

    zero = jnp.zeros((nb, LANES), F32)
    lax.fori_loop(0, nc, step, (zero,) * (2 * nk))

    state = jnp.concatenate(
        [jnp.concatenate([st_ref[k, pl.ds(b, nc, stride=nb), :] for b in range(nb)], axis=0).astype(BF16)
         for k in range(2 * nk)], axis=1)
    y = _dot(lhs, wi_ref[...]) + _dot(state, wo_ref[...])
    d = d_ref[...]
    for t in range(tc):
        yt = y[:, t * LANES:(t + 1) * LANES] + d * u_ref[step_rows(t), :]
        zs_ref[step_rows(t), :] = _gelu_tanh(yt)
    z_ref[...] = zs_ref[...].astype(z_ref.dtype)


def _ssm_scan(u, w_intra, w_in, w_out, a_step, d_skip, *, batch):
    t, d = u.shape
    seq = t // batch
    tc = SSM_TC
    nc = seq // tc
    nb = min(batch, SSM_SEQS)
    k_dim = tc * LANES
    return pl.pallas_call(
        functools.partial(_ssm_kernel, nb=nb, nc=nc),
        grid=(d // LANES, batch // nb),
        in_specs=[
            pl.BlockSpec((nb * seq, LANES), lambda j, b: (b, j)),
            pl.BlockSpec((None, k_dim, k_dim), lambda j, b: (j, 0, 0)),
            pl.BlockSpec((None, k_dim, k_dim), lambda j, b: (j, 0, 0)),
            pl.BlockSpec((None, k_dim, k_dim), lambda j, b: (j, 0, 0)),
            pl.BlockSpec((None, 2, k_dim // 2), lambda j, b: (j, 0, 0)),
            pl.BlockSpec((1, LANES), lambda j, b: (0, j)),
        ],
        out_specs=pl.BlockSpec((nb * seq, LANES), lambda j, b: (b, j)),
        out_shape=jax.ShapeDtypeStruct((t, d), BF16),
        scratch_shapes=[pltpu.VMEM((k_dim // LANES, nc * nb, LANES), F32),
                        pltpu.VMEM((nb * seq, LANES), F32)],
        compiler_params=_params("parallel", "parallel"),
        name="ssm_scan",
    )(u, w_intra, w_in, w_out, a_step, d_skip.reshape(1, d))


def _make_matmul_norm_res_kernel(glu):
    def kernel(a_ref, w_ref, x_ref, g_ref, o_ref):
        bm, d = o_ref.shape
        sub, slab = min(ROW_SUB, bm), min(COL_SLAB, d)
        g = g_ref[...]
        for r in range(bm // sub):
            rs = slice(r * sub, (r + 1) * sub)
            a = a_ref[rs, :]
            ssq = jnp.zeros((sub, 1), F32)
            for c in range(d // slab):
                cs = slice(c * slab, (c + 1) * slab)
                mix = _dot(a, w_ref[:, cs])
                if glu:
                    gate = _dot(a, w_ref[:, d + c * slab:d + (c + 1) * slab])
                    mix = mix * (1.0 / (1.0 + jnp.exp(-gate)))
                o_ref[rs, cs] = mix
                ssq = ssq + jnp.sum(mix * mix, axis=1, keepdims=True)
            o_ref[rs, :] = x_ref[rs, :] + o_ref[rs, :] * lax.rsqrt(ssq * (1.0 / d) + EPS) * g

    return kernel


def _matmul_norm_res(a, w, x, g, *, bm, name, glu=False):
    m, k = a.shape
    d = x.shape[1]
    bm = min(bm, m)
    return pl.pallas_call(
        _make_matmul_norm_res_kernel(glu),
        grid=(m // bm,),
        in_specs=[
            pl.BlockSpec((bm, k), lambda i: (i, 0)),
            pl.BlockSpec(w.shape, lambda i: (0, 0), pipeline_mode=pl.Buffered(1)),
            pl.BlockSpec((bm, d), lambda i: (i, 0)),
            pl.BlockSpec((1, d), lambda i: (0, 0)),
        ],
        out_specs=pl.BlockSpec((bm, d), lambda i: (i, 0)),
        out_shape=jax.ShapeDtypeStruct((m, d), F32),
        compiler_params=_params("parallel"),
        name=name,
    )(a, w, x, g.reshape(1, d))


def _mlp_kernel(x_ref, gpre_ref, wu_ref, wd_ref, gpost_ref, o_ref, xn_ref):
    f = pl.program_id(1)

    @pl.when(f == 0)
    def _():
        xn_ref[...] = _rms(x_ref[...], gpre_ref[...]).astype(BF16)
        o_ref[...] = jnp.zeros_like(o_ref)

    h = _dot(xn_ref[...], wu_ref[...])
    a = jnp.square(jnp.maximum(h, 0.0)).astype(BF16)
    bn = min(MLP_DOWN_COLS, o_ref.shape[1])
    for c in range(o_ref.shape[1] // bn):
        sl = slice(c * bn, (c + 1) * bn)
        o_ref[:, sl] += _dot(a, wd_ref[:, sl])

    @pl.when(f == pl.num_programs(1) - 1)
    def _():
        o_ref[...] = x_ref[...] + _rms(o_ref[...], gpost_ref[...])


def _mlp(x, g_pre, w_up, w_down, g_post, *, bm, bf):
    m, d = x.shape
    dff = w_up.shape[1]
    bm, bf = min(bm, m), min(bf, dff)
    return pl.pallas_call(
        _mlp_kernel,
        grid=(m // bm, dff // bf),
        in_specs=[
            pl.BlockSpec((bm, d), lambda i, f: (i, 0), pipeline_mode=pl.Buffered(1)),
            pl.BlockSpec((1, d), lambda i, f: (0, 0)),
            pl.BlockSpec((d, bf), lambda i, f: (0, f)),
            pl.BlockSpec((bf, d), lambda i, f: (f, 0)),
            pl.BlockSpec((1, d), lambda i, f: (0, 0)),
        ],
        out_specs=pl.BlockSpec((bm, d), lambda i, f: (i, 0)),
        out_shape=jax.ShapeDtypeStruct((m, d), F32),
        scratch_shapes=[pltpu.VMEM((bm, d), BF16)],
        compiler_params=_params("parallel", "arbitrary"),
        name="mlp",
    )(x, g_pre.reshape(1, d), w_up, w_down, g_post.reshape(1, d))


def _attn_kernel(q_ref, k_ref, v_ref, lam_ref, g_ref, o_ref, *, blk, hd, nq, lambda_init):
    qi = pl.program_id(2)
    nt = (((1,), (1,)), ((), ()))
    lv = lam_ref[...]
    lam = (jnp.exp(jnp.sum(lv[0:1] * lv[1:2], axis=1, keepdims=True))
           - jnp.exp(jnp.sum(lv[2:3] * lv[3:4], axis=1, keepdims=True)) + lambda_init)
    row = lax.broadcasted_iota(jnp.int32, (blk, blk), 0)
    col = lax.broadcasted_iota(jnp.int32, (blk, blk), 1)
    causal = col <= row

    def softmax_v(c0, kv_len):
        s = lax.dot_general(q_ref[:, c0:c0 + hd], k_ref[0:kv_len, c0:c0 + hd], nt,
                            preferred_element_type=F32)
        s_diag = jnp.where(causal, s[:, kv_len - blk:], MASK_VALUE)
        s = s_diag if kv_len == blk else jnp.concatenate([s[:, :kv_len - blk], s_diag], axis=1)
        p = jnp.exp(s - jnp.max(s, axis=1, keepdims=True))
        l = jnp.sum(p, axis=1, keepdims=True)
        return _dot(p.astype(BF16), v_ref[0:kv_len, :]) * (1.0 / l)

    for n in range(1, nq + 1):
        @pl.when(qi == n - 1)
        def _(n=n):
            o = softmax_v(0, n * blk) - lam * softmax_v(hd, n * blk)
            o_ref[...] = (_rms(o, g_ref[...]) * (1.0 - lambda_init)).astype(o_ref.dtype)


def _diff_attention(q, kv, lam_vecs, g_sub, *, batch, heads, hd, lambda_init, blk):
    t, d = q.shape
    seq = t // batch
    blk = min(blk, seq)
    nq = seq // blk
    return pl.pallas_call(
        functools.partial(_attn_kernel, blk=blk, hd=hd, nq=nq, lambda_init=lambda_init),
        grid=(batch, heads, nq),
        in_specs=[
            pl.BlockSpec((blk, 2 * hd), lambda b, h, i: (b * nq + i, h)),
            pl.BlockSpec((seq, 2 * hd), lambda b, h, i: (b, h)),
            pl.BlockSpec((seq, 2 * hd), lambda b, h, i: (b, heads + h)),
            pl.BlockSpec((8, hd), lambda b, h, i: (0, 0)),
            pl.BlockSpec((1, 2 * hd), lambda b, h, i: (0, 0)),
        ],
        out_specs=pl.BlockSpec((blk, 2 * hd), lambda b, h, i: (b * nq + i, h)),
        out_shape=jax.ShapeDtypeStruct((t, d), BF16),
        compiler_params=_params("parallel", "parallel", "arbitrary"),
        name="diff_attn",
    )(q, kv, kv, lam_vecs, g_sub.reshape(1, 2 * hd))


def _rope_tables(seq, hd):
    pos = jnp.arange(seq, dtype=F32)
    inv_freq = 1.0 / (ROPE_THETA ** (jnp.arange(0, hd, 2, dtype=F32) / hd))
    ang = pos[:, None] * inv_freq[None, :]
    emb = jnp.concatenate([ang, ang], axis=-1)
    sign = jnp.where(jnp.arange(hd) < hd // 2, -1.0, 1.0).astype(F32)
    return jnp.cos(emb), jnp.sin(emb) * sign


def kernel(x, mix_pre_g, mix_post_g, mlp_pre_g, mlp_post_g, ssm_w_in, ssm_a_re, ssm_a_im, ssm_log_dt, ssm_b_re, ssm_b_im, ssm_c_re, ssm_c_im, ssm_d, ssm_w_glu, kv_norm_g, w_kv, attn_w_q, lam_q1, lam_k1, lam_q2, lam_k2, attn_subln_g, attn_w_o, mlp_w_up, mlp_w_down):
    batch, seq, d = x.shape
    t = batch * seq
    depth = mix_pre_g.shape[0]
    n_a = ssm_w_in.shape[0]
    hd = lam_q1.shape[1]
    heads = d // (2 * hd)
    cos, sin = _rope_tables(seq, hd)

    h = x.reshape(t, d)
    kv = None
    for l in range(depth):
        if l < n_a:
            a = l
            u = _norm_matmul(h, mix_pre_g[l], ssm_w_in[a].astype(BF16), out_dtype=F32, bm=1024, name="ssm_in_proj")
            sw = _ssm_weights(ssm_a_re[a], ssm_a_im[a], ssm_log_dt[a], ssm_b_re[a], ssm_b_im[a],
                              ssm_c_re[a], ssm_c_im[a], SSM_TC)
            z = _ssm_scan(u, *sw, ssm_d[a], batch=batch)
            h = _matmul_norm_res(z, ssm_w_glu[a].astype(BF16), h, mix_post_g[l], bm=512, glu=True, name="ssm_glu_out")
        else:
            b = l - n_a
            lambda_init = 0.8 - 0.6 * math.exp(-0.3 * l)
            q = _norm_matmul(h, mix_pre_g[l], attn_w_q[b].astype(BF16), out_dtype=BF16, bm=1024, name="q_proj",
                             rope=(cos, sin), rope_cols=d, scale=hd ** -0.5)
            lam_vecs = jnp.zeros((8, hd), F32).at[0:4].set(
                jnp.stack([lam_q1[b], lam_k1[b], lam_q2[b], lam_k2[b]]).astype(F32))
            o = _diff_attention(q, kv, lam_vecs, attn_subln_g[b], batch=batch, heads=heads, hd=hd,
                                lambda_init=lambda_init, blk=512)
            h = _matmul_norm_res(o, attn_w_o[b].astype(BF16), h, mix_post_g[l], bm=1024, name="attn_out_proj")
        h = _mlp(h, mlp_pre_g[l], mlp_w_up[l].astype(BF16), mlp_w_down[l].astype(BF16), mlp_post_g[l],
                 bm=1024, bf=512)
        if l == n_a - 1:
            kv = _norm_matmul(h, kv_norm_g, w_kv.astype(BF16), out_dtype=BF16, bm=512, name="kv_proj",
                              rope=(cos, sin), rope_cols=d, scale=1.0)
    return h.reshape(batch, seq, d)
```

```python
import functools
import math

import jax
import jax.numpy as jnp
from jax import lax
from jax.experimental import pallas as pl
from jax.experimental.pallas import tpu as pltpu

EPS = 1e-6
ROPE_THETA = 10000.0
LANES = 128
SSM_TC = 8
SSM_SEQS = 4
VMEM_LIMIT_BYTES = 56 * 1024 * 1024
MLP_VMEM_LIMIT_BYTES = 58 * 1024 * 1024
MASK_VALUE = -1e30
MLP_DOWN_COLS = 512
ROW_SUB = 256
COL_SLAB = 512

F32 = jnp.float32
BF16 = jnp.bfloat16


def _params(*sem, vmem_limit_bytes=VMEM_LIMIT_BYTES):
    return pltpu.CompilerParams(dimension_semantics=sem, vmem_limit_bytes=vmem_limit_bytes)


def _rms(x, g):
    return x * lax.rsqrt(jnp.mean(x * x, axis=-1, keepdims=True) + EPS) * g


def _dot(a, b):
    return jnp.dot(a, b, preferred_element_type=F32)


def _rope(t, cos, sin_signed):
    return t * cos + pltpu.roll(t, LANES // 2, 1) * sin_signed


def _make_norm_matmul_kernel(rope_cols, scale):
    def kernel(x_ref, g_ref, w_ref, *rest):
        if rope_cols:
            cos_ref, sin_ref, o_ref = rest
        else:
            (o_ref,) = rest
        bm, n = o_ref.shape
        sub, slab = min(ROW_SUB, bm), min(COL_SLAB, n)
        g = g_ref[...]
        for r in range(bm // sub):
            rs = slice(r * sub, (r + 1) * sub)
            xn = _rms(x_ref[rs, :], g).astype(BF16)
            for c in range(n // slab):
                acc = _dot(xn, w_ref[:, c * slab:(c + 1) * slab])
                if c * slab < rope_cols:
                    cos, sin = cos_ref[rs, :], sin_ref[rs, :]
                    for cc in range(slab // LANES):
                        sl = slice(cc * LANES, (cc + 1) * LANES)
                        osl = slice(c * slab + cc * LANES, c * slab + (cc + 1) * LANES)
                        o_ref[rs, osl] = (_rope(acc[:, sl], cos, sin) * scale).astype(o_ref.dtype)
                else:
                    o_ref[rs, c * slab:(c + 1) * slab] = acc.astype(o_ref.dtype)

    return kernel


def _norm_matmul(x, g, w, *, out_dtype, bm, name, rope=None, rope_cols=0, scale=1.0):
    m, k = x.shape
    n = w.shape[1]
    bm = min(bm, m)
    args = [x, g.reshape(1, k), w]
    if rope is not None:
        cos, sin = rope
        seq = cos.shape[0]
        bm = min(bm, seq)
        assert rope_cols % min(COL_SLAB, n) == 0 and seq % bm == 0
        nseq = seq // bm
        args += [cos, sin]
    in_specs = [
        pl.BlockSpec((bm, k), lambda i: (i, 0)),
        pl.BlockSpec((1, k), lambda i: (0, 0)),
        pl.BlockSpec((k, n), lambda i: (0, 0), pipeline_mode=pl.Buffered(1)),
    ]
    if rope is not None:
        in_specs += [pl.BlockSpec((bm, LANES), lambda i: (i % nseq, 0))] * 2
    return pl.pallas_call(
        _make_norm_matmul_kernel(rope_cols, scale),
        grid=(m // bm,),
        in_specs=in_specs,
        out_specs=pl.BlockSpec((bm, n), lambda i: (i, 0)),
        out_shape=jax.ShapeDtypeStruct((m, n), out_dtype),
        compiler_params=_params("parallel"),
        name=name,
    )(*args)


def _ssm_prep_kernel(pw_ref, bb_ref, cc_ref, wi_ref, wn_ref, wo_ref, *, tc, p_ch, n_st):
    s_dim = pw_ref.shape[1]
    gpb = LANES // p_ch
    row_g = lax.shift_right_logical(lax.broadcasted_iota(jnp.int32, (LANES, s_dim), 0), int(math.log2(p_ch)))
    col_g = lax.shift_right_logical(lax.broadcasted_iota(jnp.int32, (LANES, s_dim), 1), int(math.log2(n_st)))
    same_group = row_g == col_g

    def expand(x):
        return jnp.where(same_group, jnp.concatenate([x] * gpb, axis=0), 0.0)

    b_r, b_i = expand(bb_ref[0]), expand(bb_ref[1])
    c_r, c_i = expand(cc_ref[0]), expand(cc_ref[1])
    c0 = jnp.concatenate([c_r, -c_i], axis=1).T
    kd = []
    for d in range(tc):
        pr, pi = pw_ref[d:d + 1, :], pw_ref[tc + 1 + d:tc + 2 + d, :]
        e = jnp.concatenate([pr * b_r - pi * b_i, pr * b_i + pi * b_r], axis=1)
        wn_ref[(tc - 1 - d) * LANES:(tc - d) * LANES, :] = e.astype(BF16)
        kd.append(jnp.dot(e, c0, precision=lax.Precision.HIGHEST,
                          preferred_element_type=F32).astype(BF16))
    for t in range(tc):
        pr, pi = pw_ref[t + 1:t + 2, :], pw_ref[tc + 2 + t:tc + 3 + t, :]
        e = jnp.concatenate([pr * c_r - pi * c_i, -(pr * c_i + pi * c_r)], axis=1)
        wo_ref[:, t * LANES:(t + 1) * LANES] = e.T.astype(BF16)
    zero = jnp.zeros((LANES, LANES), BF16)
    for s in range(tc):
        for t in range(tc):
            wi_ref[s * LANES:(s + 1) * LANES, t * LANES:(t + 1) * LANES] = kd[t - s] if t >= s else zero


def _ssm_weights(a_re, a_im, log_dt, b_re, b_im, c_re, c_im, tc):
    g_n, n_st = a_re.shape
    p_ch = b_re.shape[-1]
    gpb = LANES // p_ch
    nj = g_n // gpb
    s_dim = gpb * n_st
    assert p_ch & (p_ch - 1) == 0 and n_st & (n_st - 1) == 0
    step = jnp.exp(log_dt.astype(F32))[:, None]
    lam_re = jnp.minimum(a_re.astype(F32), -1e-4)
    lam_im = a_im.astype(F32)
    mag = jnp.exp(step * lam_re)
    abar_re = mag * jnp.cos(step * lam_im)
    abar_im = mag * jnp.sin(step * lam_im)
    den = lam_re * lam_re + lam_im * lam_im
    nr = abar_re - 1.0
    ni = abar_im
    coef_re = (nr * lam_re + ni * lam_im) / den
    coef_im = (ni * lam_re - nr * lam_im) / den
    bre, bim = b_re.astype(F32), b_im.astype(F32)
    bbar_re = coef_re[..., None] * bre - coef_im[..., None] * bim
    bbar_im = coef_re[..., None] * bim + coef_im[..., None] * bre
    dd = jnp.arange(tc + 1, dtype=F32)[:, None, None]
    pmag = jnp.exp(dd * (step * lam_re))
    pw = jnp.concatenate([pmag * jnp.cos(dd * (step * lam_im)),
                          pmag * jnp.sin(dd * (step * lam_im))], axis=0)
    pw = pw.reshape(2 * (tc + 1), nj, s_dim).transpose(1, 0, 2)
    bb = jnp.stack([bbar_re, bbar_im]).reshape(2, nj, gpb, n_st, p_ch)
    bb = bb.transpose(1, 0, 4, 2, 3).reshape(nj, 2, p_ch, s_dim)
    cc = jnp.stack([c_re.astype(F32), c_im.astype(F32)]).reshape(2, nj, gpb, p_ch, n_st)
    cc = cc.transpose(1, 0, 3, 2, 4).reshape(nj, 2, p_ch, s_dim)
    k_dim = tc * LANES
    wshape = jax.ShapeDtypeStruct((nj, k_dim, k_dim), BF16)
    assert 2 * s_dim == k_dim
    w_intra, w_in, w_out = pl.pallas_call(
        functools.partial(_ssm_prep_kernel, tc=tc, p_ch=p_ch, n_st=n_st),
        grid=(nj,),
        in_specs=[
            pl.BlockSpec((None, 2 * (tc + 1), s_dim), lambda j: (j, 0, 0)),
            pl.BlockSpec((None, 2, p_ch, s_dim), lambda j: (j, 0, 0, 0)),
            pl.BlockSpec((None, 2, p_ch, s_dim), lambda j: (j, 0, 0, 0)),
        ],
        out_specs=[pl.BlockSpec((None, k_dim, k_dim), lambda j: (j, 0, 0))] * 3,
        out_shape=[wshape] * 3,
        compiler_params=_params("parallel"),
        name="ssm_prep",
    )(pw, bb, cc)
    a_step = jnp.stack([pw[:, tc], pw[:, 2 * tc + 1]], axis=1)
    return w_intra, w_in, w_out, a_step


def _gelu_tanh(x):
    return 0.5 * x * (1.0 + jnp.tanh(math.sqrt(2.0 / math.pi) * (x + 0.044715 * (x * x * x))))


def _ssm_kernel(u_ref, wi_ref, wn_ref, wo_ref, a_ref, d_ref, z_ref, st_ref, zs_ref, *, nb, nc):
    tc = SSM_TC
    rows = nb * nc
    nk = st_ref.shape[0] // 2

    def step_rows(s):
        return pl.ds(s, rows, stride=tc)

    lhs = jnp.concatenate([u_ref[step_rows(s), :].astype(BF16) for s in range(tc)], axis=1)
    contrib = _dot(lhs, wn_ref[...])
    for k in range(2 * nk):
        for b in range(nb):
            st_ref[k, pl.ds(b, nc, stride=nb), :] = contrib[b * nc:(b + 1) * nc, k * LANES:(k + 1) * LANES]

    a = a_ref[...]
    ar = [jnp.broadcast_to(a[0:1, k * LANES:(k + 1) * LANES], (nb, LANES)) for k in range(nk)]
    ai = [jnp.broadcast_to(a[1:2, k * LANES:(k + 1) * LANES], (nb, LANES)) for k in range(nk)]

    def step(c, carry):
        rws = pl.ds(pl.multiple_of(c * nb, nb), nb)
        out = []
        for k in range(nk):
            sr, si = carry[2 * k], carry[2 * k + 1]
            cr = st_ref[k, rws, :]
            ci = st_ref[nk + k, rws, :]
            st_ref[k, rws, :] = sr
            st_ref[nk + k, rws, :] = si
            out += [ar[k] * sr - ai[k] * si + cr, ar[k] * si + ai[k] * sr + ci]
        return tuple(out)

    zero = jnp.zeros((nb, LANES), F32)
    lax.fori_loop(0, nc, step, (zero,) * (2 * nk), unroll=True)

    state = jnp.concatenate(
        [jnp.concatenate([st_ref[k, pl.ds(b, nc, stride=nb), :] for b in range(nb)], axis=0).astype(BF16)
         for k in range(2 * nk)], axis=1)
    y = _dot(lhs, wi_ref[...]) + _dot(state, wo_ref[...])
    d = d_ref[...]
    for t in range(tc):
        yt = y[:, t * LANES:(t + 1) * LANES] + d * u_ref[step_rows(t), :]
        zs_ref[step_rows(t), :] = _gelu_tanh(yt)
    z_ref[...] = zs_ref[...].astype(z_ref.dtype)


def _ssm_scan(u, w_intra, w_in, w_out, a_step, d_skip, *, batch):
    t, d = u.shape
    seq = t // batch
    tc = SSM_TC
    nc = seq // tc
    nb = min(batch, SSM_SEQS)
    k_dim = tc * LANES
    return pl.pallas_call(
        functools.partial(_ssm_kernel, nb=nb, nc=nc),
        grid=(d // LANES, batch // nb),
        in_specs=[
            pl.BlockSpec((nb * seq, LANES), lambda j, b: (b, j)),
            pl.BlockSpec((None, k_dim, k_dim), lambda j, b: (j, 0, 0)),
            pl.BlockSpec((None, k_dim, k_dim), lambda j, b: (j, 0, 0)),
            pl.BlockSpec((None, k_dim, k_dim), lambda j, b: (j, 0, 0)),
            pl.BlockSpec((None, 2, k_dim // 2), lambda j, b: (j, 0, 0)),
            pl.BlockSpec((1, LANES), lambda j, b: (0, j)),
        ],
        out_specs=pl.BlockSpec((nb * seq, LANES), lambda j, b: (b, j)),
        out_shape=jax.ShapeDtypeStruct((t, d), BF16),
        scratch_shapes=[pltpu.VMEM((k_dim // LANES, nc * nb, LANES), F32),
                        pltpu.VMEM((nb * seq, LANES), F32)],
        compiler_params=_params("parallel", "parallel"),
        name="ssm_scan",
    )(u, w_intra, w_in, w_out, a_step, d_skip.reshape(1, d))


def _make_matmul_norm_res_kernel(glu):
    def kernel(a_ref, w_ref, x_ref, g_ref, o_ref):
        bm, d = o_ref.shape
        sub, slab = min(ROW_SUB, bm), min(COL_SLAB, d)
        g = g_ref[...]
        for r in range(bm // sub):
            rs = slice(r * sub, (r + 1) * sub)
            a = a_ref[rs, :]
            ssq = jnp.zeros((sub, 1), F32)
            for c in range(d // slab):
                cs = slice(c * slab, (c + 1) * slab)
                mix = _dot(a, w_ref[:, cs])
                if glu:
                    gate = _dot(a, w_ref[:, d + c * slab:d + (c + 1) * slab])
                    mix = mix * (1.0 / (1.0 + jnp.exp(-gate)))
                o_ref[rs, cs] = mix
                ssq = ssq + jnp.sum(mix * mix, axis=1, keepdims=True)
            o_ref[rs, :] = x_ref[rs, :] + o_ref[rs, :] * lax.rsqrt(ssq * (1.0 / d) + EPS) * g

    return kernel


def _matmul_norm_res(a, w, x, g, *, bm, name, glu=False):
    m, k = a.shape
    d = x.shape[1]
    bm = min(bm, m)
    return pl.pallas_call(
        _make_matmul_norm_res_kernel(glu),
        grid=(m // bm,),
        in_specs=[
            pl.BlockSpec((bm, k), lambda i: (i, 0)),
            pl.BlockSpec(w.shape, lambda i: (0, 0), pipeline_mode=pl.Buffered(1)),
            pl.BlockSpec((bm, d), lambda i: (i, 0)),
            pl.BlockSpec((1, d), lambda i: (0, 0)),
        ],
        out_specs=pl.BlockSpec((bm, d), lambda i: (i, 0)),
        out_shape=jax.ShapeDtypeStruct((m, d), F32),
        compiler_params=_params("parallel"),
        name=name,
    )(a, w, x, g.reshape(1, d))


def _mlp_kernel(x_ref, gpre_ref, wu_ref, wd_ref, gpost_ref, o_ref, xn_ref):
    f = pl.program_id(1)

    @pl.when(f == 0)
    def _():
        xn_ref[...] = _rms(x_ref[...], gpre_ref[...]).astype(BF16)
        o_ref[...] = jnp.zeros_like(o_ref)

    h = _dot(xn_ref[...], wu_ref[...])
    a = jnp.square(jnp.maximum(h, 0.0)).astype(BF16)
    bn = min(MLP_DOWN_COLS, o_ref.shape[1])
    for c in range(o_ref.shape[1] // bn):
        sl = slice(c * bn, (c + 1) * bn)
        o_ref[:, sl] += _dot(a, wd_ref[:, sl])

    @pl.when(f == pl.num_programs(1) - 1)
    def _():
        o_ref[...] = x_ref[...] + _rms(o_ref[...], gpost_ref[...])


def _mlp(x, g_pre, w_up, w_down, g_post, *, bm, bf):
    m, d = x.shape
    dff = w_up.shape[1]
    bm, bf = min(bm, m), min(bf, dff)
    return pl.pallas_call(
        _mlp_kernel,
        grid=(m // bm, dff // bf),
        in_specs=[
            pl.BlockSpec((bm, d), lambda i, f: (i, 0), pipeline_mode=pl.Buffered(1)),
            pl.BlockSpec((1, d), lambda i, f: (0, 0)),
            pl.BlockSpec((d, bf), lambda i, f: (0, f)),
            pl.BlockSpec((bf, d), lambda i, f: (f, 0)),
            pl.BlockSpec((1, d), lambda i, f: (0, 0)),
        ],
        out_specs=pl.BlockSpec((bm, d), lambda i, f: (i, 0)),
        out_shape=jax.ShapeDtypeStruct((m, d), F32),
        scratch_shapes=[pltpu.VMEM((bm, d), BF16)],
        compiler_params=_params("parallel", "arbitrary", vmem_limit_bytes=MLP_VMEM_LIMIT_BYTES),
        name="mlp",
    )(x, g_pre.reshape(1, d), w_up, w_down, g_post.reshape(1, d))


def _attn_kernel(q_ref, k_ref, v_ref, lam_ref, g_ref, o_ref, *, blk, hd, nq, lambda_init):
    qi = pl.program_id(2)
    nt = (((1,), (1,)), ((), ()))
    lv = lam_ref[...]
    lam = (jnp.exp(jnp.sum(lv[0:1] * lv[1:2], axis=1, keepdims=True))
           - jnp.exp(jnp.sum(lv[2:3] * lv[3:4], axis=1, keepdims=True)) + lambda_init)
    row = lax.broadcasted_iota(jnp.int32, (blk, blk), 0)
    col = lax.broadcasted_iota(jnp.int32, (blk, blk), 1)
    causal = col <= row

    def softmax_v(c0, kv_len):
        s = lax.dot_general(q_ref[:, c0:c0 + hd], k_ref[0:kv_len, c0:c0 + hd], nt,
                            preferred_element_type=F32)
        s_diag = jnp.where(causal, s[:, kv_len - blk:], MASK_VALUE)
        s = s_diag if kv_len == blk else jnp.concatenate([s[:, :kv_len - blk], s_diag], axis=1)
        p = jnp.exp(s - jnp.max(s, axis=1, keepdims=True))
        l = jnp.sum(p, axis=1, keepdims=True)
        return _dot(p.astype(BF16), v_ref[0:kv_len, :]) * (1.0 / l)

    for n in range(1, nq + 1):
        @pl.when(qi == n - 1)
        def _(n=n):
            o = softmax_v(0, n * blk) - lam * softmax_v(hd, n * blk)
            o_ref[...] = (_rms(o, g_ref[...]) * (1.0 - lambda_init)).astype(o_ref.dtype)


def _diff_attention(q, kv, lam_vecs, g_sub, *, batch, heads, hd, lambda_init, blk):
    t, d = q.shape
    seq = t // batch
    blk = min(blk, seq)
    nq = seq // blk
    return pl.pallas_call(
        functools.partial(_attn_kernel, blk=blk, hd=hd, nq=nq, lambda_init=lambda_init),
        grid=(batch, heads, nq),
        in_specs=[
            pl.BlockSpec((blk, 2 * hd), lambda b, h, i: (b * nq + i, h)),
            pl.BlockSpec((seq, 2 * hd), lambda b, h, i: (b, h)),
            pl.BlockSpec((seq, 2 * hd), lambda b, h, i: (b, heads + h)),
            pl.BlockSpec((8, hd), lambda b, h, i: (0, 0)),
            pl.BlockSpec((1, 2 * hd), lambda b, h, i: (0, 0)),
        ],
        out_specs=pl.BlockSpec((blk, 2 * hd), lambda b, h, i: (b * nq + i, h)),
        out_shape=jax.ShapeDtypeStruct((t, d), BF16),
        compiler_params=_params("parallel", "parallel", "arbitrary"),
        name="diff_attn",
    )(q, kv, kv, lam_vecs, g_sub.reshape(1, 2 * hd))


def _rope_tables(seq, hd):
    pos = jnp.arange(seq, dtype=F32)
    inv_freq = 1.0 / (ROPE_THETA ** (jnp.arange(0, hd, 2, dtype=F32) / hd))
    ang = pos[:, None] * inv_freq[None, :]
    emb = jnp.concatenate([ang, ang], axis=-1)
    sign = jnp.where(jnp.arange(hd) < hd // 2, -1.0, 1.0).astype(F32)
    return jnp.cos(emb), jnp.sin(emb) * sign


def kernel(x, mix_pre_g, mix_post_g, mlp_pre_g, mlp_post_g, ssm_w_in, ssm_a_re, ssm_a_im, ssm_log_dt, ssm_b_re, ssm_b_im, ssm_c_re, ssm_c_im, ssm_d, ssm_w_glu, kv_norm_g, w_kv, attn_w_q, lam_q1, lam_k1, lam_q2, lam_k2, attn_subln_g, attn_w_o, mlp_w_up, mlp_w_down):
    batch, seq, d = x.shape
    t = batch * seq
    depth = mix_pre_g.shape[0]
    n_a = ssm_w_in.shape[0]
    hd = lam_q1.shape[1]
    heads = d // (2 * hd)
    cos, sin = _rope_tables(seq, hd)

    h = x.reshape(t, d)
    kv = None
    for l in range(depth):
        if l < n_a:
            a = l
            u = _norm_matmul(h, mix_pre_g[l], ssm_w_in[a].astype(BF16), out_dtype=F32, bm=1024, name="ssm_in_proj")
            sw = _ssm_weights(ssm_a_re[a], ssm_a_im[a], ssm_log_dt[a], ssm_b_re[a], ssm_b_im[a],
                              ssm_c_re[a], ssm_c_im[a], SSM_TC)
            z = _ssm_scan(u, *sw, ssm_d[a], batch=batch)
            h = _matmul_norm_res(z, ssm_w_glu[a].astype(BF16), h, mix_post_g[l], bm=512, glu=True, name="ssm_glu_out")
        else:
            b = l - n_a
            lambda_init = 0.8 - 0.6 * math.exp(-0.3 * l)
            q = _norm_matmul(h, mix_pre_g[l], attn_w_q[b].astype(BF16), out_dtype=BF16, bm=1024, name="q_proj",
                             rope=(cos, sin), rope_cols=d, scale=hd ** -0.5)
            lam_vecs = jnp.zeros((8, hd), F32).at[0:4].set(
                jnp.stack([lam_q1[b], lam_k1[b], lam_q2[b], lam_k2[b]]).astype(F32))
            o = _diff_attention(q, kv, lam_vecs, attn_subln_g[b], batch=batch, heads=heads, hd=hd,
                                lambda_init=lambda_init, blk=512)
            h = _matmul_norm_res(o, attn_w_o[b].astype(BF16), h, mix_post_g[l], bm=1024, name="attn_out_proj")
        h = _mlp(h, mlp_pre_g[l], mlp_w_up[l].astype(BF16), mlp_w_down[l].astype(BF16), mlp_post_g[l],
                 bm=1024, bf=1024)
        if l == n_a - 1:
            kv = _norm_matmul(h, kv_norm_g, w_kv.astype(BF16), out_dtype=BF16, bm=512, name="kv_proj",
                              rope=(cos, sin), rope_cols=d, scale=1.0)
    return h.reshape(batch, seq, d)
```

```python
import functools
import math

import jax
import jax.numpy as jnp
from jax import lax
from jax.experimental import pallas as pl
from jax.experimental.pallas import tpu as pltpu

EPS = 1e-6
ROPE_THETA = 10000.0
LANES = 128
SSM_TC = 8
SSM_SEQS = 4
VMEM_LIMIT_BYTES = 56 * 1024 * 1024
MLP_VMEM_LIMIT_BYTES = 58 * 1024 * 1024
MASK_VALUE = -1e30
MLP_DOWN_COLS = 512
ROW_SUB = 256
COL_SLAB = 512
CAST_BLOCK_ELEMS = 2 * 1024 * 1024

F32 = jnp.float32
BF16 = jnp.bfloat16


def _params(*sem, vmem_limit_bytes=VMEM_LIMIT_BYTES):
    return pltpu.CompilerParams(dimension_semantics=sem, vmem_limit_bytes=vmem_limit_bytes)


def _rms(x, g):
    return x * lax.rsqrt(jnp.mean(x * x, axis=-1, keepdims=True) + EPS) * g


def _dot(a, b):
    return jnp.dot(a, b, preferred_element_type=F32)


def _rope(t, cos, sin_signed):
    return t * cos + pltpu.roll(t, LANES // 2, 1) * sin_signed


def _make_norm_matmul_kernel(rope_cols, scale):
    def kernel(x_ref, g_ref, w_ref, *rest):
        if rope_cols:
            cos_ref, sin_ref, o_ref = rest
        else:
            (o_ref,) = rest
        bm, n = o_ref.shape
        sub, slab = min(ROW_SUB, bm), min(COL_SLAB, n)
        g = g_ref[...]
        for r in range(bm // sub):
            rs = slice(r * sub, (r + 1) * sub)
            xn = _rms(x_ref[rs, :], g).astype(BF16)
            for c in range(n // slab):
                acc = _dot(xn, w_ref[:, c * slab:(c + 1) * slab])
                if c * slab < rope_cols:
                    cos, sin = cos_ref[rs, :], sin_ref[rs, :]
                    for cc in range(slab // LANES):
                        sl = slice(cc * LANES, (cc + 1) * LANES)
                        osl = slice(c * slab + cc * LANES, c * slab + (cc + 1) * LANES)
                        o_ref[rs, osl] = (_rope(acc[:, sl], cos, sin) * scale).astype(o_ref.dtype)
                else:
                    o_ref[rs, c * slab:(c + 1) * slab] = acc.astype(o_ref.dtype)

    return kernel


def _norm_matmul(x, g, w, *, out_dtype, bm, name, rope=None, rope_cols=0, scale=1.0):
    m, k = x.shape
    n = w.shape[1]
    bm = min(bm, m)
    args = [x, g.reshape(1, k), w]
    if rope is not None:
        cos, sin = rope
        seq = cos.shape[0]
        bm = min(bm, seq)
        assert rope_cols % min(COL_SLAB, n) == 0 and seq % bm == 0
        nseq = seq // bm
        args += [cos, sin]
    in_specs = [
        pl.BlockSpec((bm, k), lambda i: (i, 0)),
        pl.BlockSpec((1, k), lambda i: (0, 0)),
        pl.BlockSpec((k, n), lambda i: (0, 0), pipeline_mode=pl.Buffered(1)),
    ]
    if rope is not None:
        in_specs += [pl.BlockSpec((bm, LANES), lambda i: (i % nseq, 0))] * 2
    return pl.pallas_call(
        _make_norm_matmul_kernel(rope_cols, scale),
        grid=(m // bm,),
        in_specs=in_specs,
        out_specs=pl.BlockSpec((bm, n), lambda i: (i, 0)),
        out_shape=jax.ShapeDtypeStruct((m, n), out_dtype),
        compiler_params=_params("parallel"),
        name=name,
    )(*args)


def _ssm_prep_kernel(pw_ref, bb_ref, cc_ref, wi_ref, wn_ref, wo_ref, *, tc, p_ch, n_st):
    s_dim = pw_ref.shape[1]
    gpb = LANES // p_ch
    row_g = lax.shift_right_logical(lax.broadcasted_iota(jnp.int32, (LANES, s_dim), 0), int(math.log2(p_ch)))
    col_g = lax.shift_right_logical(lax.broadcasted_iota(jnp.int32, (LANES, s_dim), 1), int(math.log2(n_st)))
    same_group = row_g == col_g

    def expand(x):
        return jnp.where(same_group, jnp.concatenate([x] * gpb, axis=0), 0.0)

    b_r, b_i = expand(bb_ref[0]), expand(bb_ref[1])
    c_r, c_i = expand(cc_ref[0]), expand(cc_ref[1])
    c0 = jnp.concatenate([c_r, -c_i], axis=1).T
    kd = []
    for d in range(tc):
        pr, pi = pw_ref[d:d + 1, :], pw_ref[tc + 1 + d:tc + 2 + d, :]
        e = jnp.concatenate([pr * b_r - pi * b_i, pr * b_i + pi * b_r], axis=1)
        wn_ref[(tc - 1 - d) * LANES:(tc - d) * LANES, :] = e.astype(BF16)
        kd.append(jnp.dot(e, c0, precision=lax.Precision.HIGHEST,
                          preferred_element_type=F32).astype(BF16))
    for t in range(tc):
        pr, pi = pw_ref[t + 1:t + 2, :], pw_ref[tc + 2 + t:tc + 3 + t, :]
        e = jnp.concatenate([pr * c_r - pi * c_i, -(pr * c_i + pi * c_r)], axis=1)
        wo_ref[:, t * LANES:(t + 1) * LANES] = e.T.astype(BF16)
    zero = jnp.zeros((LANES, LANES), BF16)
    for s in range(tc):
        for t in range(tc):
            wi_ref[s * LANES:(s + 1) * LANES, t * LANES:(t + 1) * LANES] = kd[t - s] if t >= s else zero


def _ssm_weights(a_re, a_im, log_dt, b_re, b_im, c_re, c_im, tc):
    g_n, n_st = a_re.shape
    p_ch = b_re.shape[-1]
    gpb = LANES // p_ch
    nj = g_n // gpb
    s_dim = gpb * n_st
    assert p_ch & (p_ch - 1) == 0 and n_st & (n_st - 1) == 0
    step = jnp.exp(log_dt.astype(F32))[:, None]
    lam_re = jnp.minimum(a_re.astype(F32), -1e-4)
    lam_im = a_im.astype(F32)
    mag = jnp.exp(step * lam_re)
    abar_re = mag * jnp.cos(step * lam_im)
    abar_im = mag * jnp.sin(step * lam_im)
    den = lam_re * lam_re + lam_im * lam_im
    nr = abar_re - 1.0
    ni = abar_im
    coef_re = (nr * lam_re + ni * lam_im) / den
    coef_im = (ni * lam_re - nr * lam_im) / den
    bre, bim = b_re.astype(F32), b_im.astype(F32)
    bbar_re = coef_re[..., None] * bre - coef_im[..., None] * bim
    bbar_im = coef_re[..., None] * bim + coef_im[..., None] * bre
    dd = jnp.arange(tc + 1, dtype=F32)[:, None, None]
    pmag = jnp.exp(dd * (step * lam_re))
    pw = jnp.concatenate([pmag * jnp.cos(dd * (step * lam_im)),
                          pmag * jnp.sin(dd * (step * lam_im))], axis=0)
    pw = pw.reshape(2 * (tc + 1), nj, s_dim).transpose(1, 0, 2)
    bb = jnp.stack([bbar_re, bbar_im]).reshape(2, nj, gpb, n_st, p_ch)
    bb = bb.transpose(1, 0, 4, 2, 3).reshape(nj, 2, p_ch, s_dim)
    cc = jnp.stack([c_re.astype(F32), c_im.astype(F32)]).reshape(2, nj, gpb, p_ch, n_st)
    cc = cc.transpose(1, 0, 3, 2, 4).reshape(nj, 2, p_ch, s_dim)
    k_dim = tc * LANES
    wshape = jax.ShapeDtypeStruct((nj, k_dim, k_dim), BF16)
    assert 2 * s_dim == k_dim
    w_intra, w_in, w_out = pl.pallas_call(
        functools.partial(_ssm_prep_kernel, tc=tc, p_ch=p_ch, n_st=n_st),
        grid=(nj,),
        in_specs=[
            pl.BlockSpec((None, 2 * (tc + 1), s_dim), lambda j: (j, 0, 0)),
            pl.BlockSpec((None, 2, p_ch, s_dim), lambda j: (j, 0, 0, 0)),
            pl.BlockSpec((None, 2, p_ch, s_dim), lambda j: (j, 0, 0, 0)),
        ],
        out_specs=[pl.BlockSpec((None, k_dim, k_dim), lambda j: (j, 0, 0))] * 3,
        out_shape=[wshape] * 3,
        compiler_params=_params("parallel"),
        name="ssm_prep",
    )(pw, bb, cc)
    a_step = jnp.stack([pw[:, tc], pw[:, 2 * tc + 1]], axis=1)
    return w_intra, w_in, w_out, a_step


def _gelu_tanh(x):
    return 0.5 * x * (1.0 + jnp.tanh(math.sqrt(2.0 / math.pi) * (x + 0.044715 * (x * x * x))))


def _ssm_kernel(u_ref, wi_ref, wn_ref, wo_ref, a_ref, d_ref, z_ref, st_ref, zs_ref, *, nb, nc):
    tc = SSM_TC
    rows = nb * nc
    nk = st_ref.shape[0] // 2

    def step_rows(s):
        return pl.ds(s, rows, stride=tc)

    lhs = jnp.concatenate([u_ref[step_rows(s), :].astype(BF16) for s in range(tc)], axis=1)
    contrib = _dot(lhs, wn_ref[...])
    for k in range(2 * nk):
        for b in range(nb):
            st_ref[k, pl.ds(b, nc, stride=nb), :] = contrib[b * nc:(b + 1) * nc, k * LANES:(k + 1) * LANES]

    a = a_ref[...]
    ar = [jnp.broadcast_to(a[0:1, k * LANES:(k + 1) * LANES], (nb, LANES)) for k in range(nk)]
    ai = [jnp.broadcast_to(a[1:2, k * LANES:(k + 1) * LANES], (nb, LANES)) for k in range(nk)]

    def step(c, carry):
        rws = pl.ds(pl.multiple_of(c * nb, nb), nb)
        out = []
        for k in range(nk):
            sr, si = carry[2 * k], carry[2 * k + 1]
            cr = st_ref[k, rws, :]
            ci = st_ref[nk + k, rws, :]
            st_ref[k, rws, :] = sr
            st_ref[nk + k, rws, :] = si
            out += [ar[k] * sr - ai[k] * si + cr, ar[k] * si + ai[k] * sr + ci]
        return tuple(out)

    zero = jnp.zeros((nb, LANES), F32)
    lax.fori_loop(0, nc, step, (zero,) * (2 * nk), unroll=True)

    state = jnp.concatenate(
        [jnp.concatenate([st_ref[k, pl.ds(b, nc, stride=nb), :] for b in range(nb)], axis=0).astype(BF16)
         for k in range(2 * nk)], axis=1)
    y = _dot(lhs, wi_ref[...]) + _dot(state, wo_ref[...])
    d = d_ref[...]
    for t in range(tc):
        yt = y[:, t * LANES:(t + 1) * LANES] + d * u_ref[step_rows(t), :]
        zs_ref[step_rows(t), :] = _gelu_tanh(yt)
    z_ref[...] = zs_ref[...].astype(z_ref.dtype)


def _ssm_scan(u, w_intra, w_in, w_out, a_step, d_skip, *, batch):
    t, d = u.shape
    seq = t // batch
    tc = SSM_TC
    nc = seq // tc
    nb = min(batch, SSM_SEQS)
    k_dim = tc * LANES
    return pl.pallas_call(
        functools.partial(_ssm_kernel, nb=nb, nc=nc),
        grid=(d // LANES, batch // nb),
        in_specs=[
            pl.BlockSpec((nb * seq, LANES), lambda j, b: (b, j)),
            pl.BlockSpec((None, k_dim, k_dim), lambda j, b: (j, 0, 0)),
            pl.BlockSpec((None, k_dim, k_dim), lambda j, b: (j, 0, 0)),
            pl.BlockSpec((None, k_dim, k_dim), lambda j, b: (j, 0, 0)),
            pl.BlockSpec((None, 2, k_dim // 2), lambda j, b: (j, 0, 0)),
            pl.BlockSpec((1, LANES), lambda j, b: (0, j)),
        ],
        out_specs=pl.BlockSpec((nb * seq, LANES), lambda j, b: (b, j)),
        out_shape=jax.ShapeDtypeStruct((t, d), BF16),
        scratch_shapes=[pltpu.VMEM((k_dim // LANES, nc * nb, LANES), F32),
                        pltpu.VMEM((nb * seq, LANES), F32)],
        compiler_params=_params("parallel", "parallel"),
        name="ssm_scan",
    )(u, w_intra, w_in, w_out, a_step, d_skip.reshape(1, d))


def _make_matmul_norm_res_kernel(glu):
    def kernel(a_ref, w_ref, x_ref, g_ref, o_ref):
        bm, d = o_ref.shape
        sub, slab = min(ROW_SUB, bm), min(COL_SLAB, d)
        g = g_ref[...]
        for r in range(bm // sub):
            rs = slice(r * sub, (r + 1) * sub)
            a = a_ref[rs, :]
            ssq = jnp.zeros((sub, 1), F32)
            for c in range(d // slab):
                cs = slice(c * slab, (c + 1) * slab)
                mix = _dot(a, w_ref[:, cs])
                if glu:
                    gate = _dot(a, w_ref[:, d + c * slab:d + (c + 1) * slab])
                    mix = mix * (1.0 / (1.0 + jnp.exp(-gate)))
                o_ref[rs, cs] = mix
                ssq = ssq + jnp.sum(mix * mix, axis=1, keepdims=True)
            o_ref[rs, :] = x_ref[rs, :] + o_ref[rs, :] * lax.rsqrt(ssq * (1.0 / d) + EPS) * g

    return kernel


def _matmul_norm_res(a, w, x, g, *, bm, name, glu=False):
    m, k = a.shape
    d = x.shape[1]
    bm = min(bm, m)
    return pl.pallas_call(
        _make_matmul_norm_res_kernel(glu),
        grid=(m // bm,),
        in_specs=[
            pl.BlockSpec((bm, k), lambda i: (i, 0)),
            pl.BlockSpec(w.shape, lambda i: (0, 0), pipeline_mode=pl.Buffered(1)),
            pl.BlockSpec((bm, d), lambda i: (i, 0)),
            pl.BlockSpec((1, d), lambda i: (0, 0)),
        ],
        out_specs=pl.BlockSpec((bm, d), lambda i: (i, 0)),
        out_shape=jax.ShapeDtypeStruct((m, d), F32),
        compiler_params=_params("parallel"),
        name=name,
    )(a, w, x, g.reshape(1, d))


def _mlp_kernel(x_ref, gpre_ref, wu_ref, wd_ref, gpost_ref, o_ref, xn_ref, *, nf):
    f = pl.program_id(1)
    bm, d = o_ref.shape
    bn = min(MLP_DOWN_COLS, d)

    def up_down(xn, rs, first):
        a = jnp.square(jnp.maximum(_dot(xn, wu_ref[...]), 0.0)).astype(BF16)
        for c in range(d // bn):
            sl = slice(c * bn, (c + 1) * bn)
            part = _dot(a, wd_ref[:, sl])
            if first:
                o_ref[rs, sl] = part
            else:
                o_ref[rs, sl] += part

    def edge_step(first, last):
        sub = min(ROW_SUB, bm)
        for r in range(bm // sub):
            rs = slice(r * sub, (r + 1) * sub)
            if first:
                xn = _rms(x_ref[rs, :], gpre_ref[...]).astype(BF16)
                xn_ref[rs, :] = xn
            else:
                xn = xn_ref[rs, :]
            up_down(xn, rs, first)
            if last:
                o_ref[rs, :] = x_ref[rs, :] + _rms(o_ref[rs, :], gpost_ref[...])

    @pl.when(f == 0)
    def _():
        edge_step(True, nf == 1)

    if nf > 2:
        @pl.when(jnp.logical_and(f > 0, f < nf - 1))
        def _():
            up_down(xn_ref[...], slice(None), False)

    if nf > 1:
        @pl.when(f == nf - 1)
        def _():
            edge_step(False, True)


def _mlp(x, g_pre, w_up, w_down, g_post, *, bm, bf):
    m, d = x.shape
    dff = w_up.shape[1]
    bm, bf = min(bm, m), min(bf, dff)
    return pl.pallas_call(
        functools.partial(_mlp_kernel, nf=dff // bf),
        grid=(m // bm, dff // bf),
        in_specs=[
            pl.BlockSpec((bm, d), lambda i, f: (i, 0), pipeline_mode=pl.Buffered(1)),
            pl.BlockSpec((1, d), lambda i, f: (0, 0)),
            pl.BlockSpec((d, bf), lambda i, f: (0, f)),
            pl.BlockSpec((bf, d), lambda i, f: (f, 0)),
            pl.BlockSpec((1, d), lambda i, f: (0, 0)),
        ],
        out_specs=pl.BlockSpec((bm, d), lambda i, f: (i, 0)),
        out_shape=jax.ShapeDtypeStruct((m, d), F32),
        scratch_shapes=[pltpu.VMEM((bm, d), BF16)],
        compiler_params=_params("parallel", "arbitrary", vmem_limit_bytes=MLP_VMEM_LIMIT_BYTES),
        name="mlp",
    )(x, g_pre.reshape(1, d), w_up, w_down, g_post.reshape(1, d))


def _attn_kernel(q_ref, k_ref, v_ref, lam_ref, g_ref, o_ref, *, blk, hd, nq, lambda_init):
    qi = pl.program_id(2)
    nt = (((1,), (1,)), ((), ()))
    lv = lam_ref[...]
    lam = (jnp.exp(jnp.sum(lv[0:1] * lv[1:2], axis=1, keepdims=True))
           - jnp.exp(jnp.sum(lv[2:3] * lv[3:4], axis=1, keepdims=True)) + lambda_init)
    row = lax.broadcasted_iota(jnp.int32, (blk, blk), 0)
    col = lax.broadcasted_iota(jnp.int32, (blk, blk), 1)
    causal = col <= row

    def softmax_v(c0, kv_len):
        outs = []
        for r0 in range(0, blk, ROW_SUB):
            rs = slice(r0, r0 + ROW_SUB)
            s = lax.dot_general(q_ref[rs, c0:c0 + hd], k_ref[0:kv_len, c0:c0 + hd], nt,
                                preferred_element_type=F32)
            s_diag = jnp.where(causal[rs, :], s[:, kv_len - blk:], MASK_VALUE)
            s = s_diag if kv_len == blk else jnp.concatenate([s[:, :kv_len - blk], s_diag], axis=1)
            p = jnp.exp(s - jnp.max(s, axis=1, keepdims=True))
            l = jnp.sum(p, axis=1, keepdims=True)
            outs.append(_dot(p.astype(BF16), v_ref[0:kv_len, :]) * (1.0 / l))
        return jnp.concatenate(outs, axis=0)

    for n in range(1, nq + 1):
        @pl.when(qi == n - 1)
        def _(n=n):
            o = softmax_v(0, n * blk) - lam * softmax_v(hd, n * blk)
            o_ref[...] = (_rms(o, g_ref[...]) * (1.0 - lambda_init)).astype(o_ref.dtype)


def _diff_attention(q, kv, lam_vecs, g_sub, *, batch, heads, hd, lambda_init, blk):
    t, d = q.shape
    seq = t // batch
    blk = min(blk, seq)
    nq = seq // blk
    return pl.pallas_call(
        functools.partial(_attn_kernel, blk=blk, hd=hd, nq=nq, lambda_init=lambda_init),
        grid=(batch, heads, nq),
        in_specs=[
            pl.BlockSpec((blk, 2 * hd), lambda b, h, i: (b * nq + i, h)),
            pl.BlockSpec((seq, 2 * hd), lambda b, h, i: (b, h)),
            pl.BlockSpec((seq, 2 * hd), lambda b, h, i: (b, heads + h)),
            pl.BlockSpec((8, hd), lambda b, h, i: (0, 0)),
            pl.BlockSpec((1, 2 * hd), lambda b, h, i: (0, 0)),
        ],
        out_specs=pl.BlockSpec((blk, 2 * hd), lambda b, h, i: (b * nq + i, h)),
        out_shape=jax.ShapeDtypeStruct((t, d), BF16),
        compiler_params=_params("parallel", "parallel", "arbitrary"),
        name="diff_attn",
    )(q, kv, kv, lam_vecs, g_sub.reshape(1, 2 * hd))


def _cast_kernel(w_ref, o_ref):
    o_ref[...] = w_ref[...].astype(o_ref.dtype)


def _to_bf16(w, layer=None):
    if layer is None:
        w, layer = w[None], 0
    _, k, n = w.shape
    rows = max(8, min(k, CAST_BLOCK_ELEMS // n))
    assert k % rows == 0
    return pl.pallas_call(
        _cast_kernel,
        grid=(k // rows,),
        in_specs=[pl.BlockSpec((None, rows, n), lambda i: (layer, i, 0))],
        out_specs=pl.BlockSpec((rows, n), lambda i: (i, 0)),
        out_shape=jax.ShapeDtypeStruct((k, n), BF16),
        compiler_params=_params("parallel"),
        name="cast_bf16",
    )(w)


def _rope_tables(seq, hd):
    pos = jnp.arange(seq, dtype=F32)
    inv_freq = 1.0 / (ROPE_THETA ** (jnp.arange(0, hd, 2, dtype=F32) / hd))
    ang = pos[:, None] * inv_freq[None, :]
    emb = jnp.concatenate([ang, ang], axis=-1)
    sign = jnp.where(jnp.arange(hd) < hd // 2, -1.0, 1.0).astype(F32)
    return jnp.cos(emb), jnp.sin(emb) * sign


def kernel(x, mix_pre_g, mix_post_g, mlp_pre_g, mlp_post_g, ssm_w_in, ssm_a_re, ssm_a_im, ssm_log_dt, ssm_b_re, ssm_b_im, ssm_c_re, ssm_c_im, ssm_d, ssm_w_glu, kv_norm_g, w_kv, attn_w_q, lam_q1, lam_k1, lam_q2, lam_k2, attn_subln_g, attn_w_o, mlp_w_up, mlp_w_down):
    batch, seq, d = x.shape
    t = batch * seq
    depth = mix_pre_g.shape[0]
    n_a = ssm_w_in.shape[0]
    hd = lam_q1.shape[1]
    heads = d // (2 * hd)
    cos, sin = _rope_tables(seq, hd)

    h = x.reshape(t, d)
    kv = None
    for l in range(depth):
        if l < n_a:
            a = l
            u = _norm_matmul(h, mix_pre_g[l], _to_bf16(ssm_w_in, a), out_dtype=F32, bm=1024, name="ssm_in_proj")
            sw = _ssm_weights(ssm_a_re[a], ssm_a_im[a], ssm_log_dt[a], ssm_b_re[a], ssm_b_im[a],
                              ssm_c_re[a], ssm_c_im[a], SSM_TC)
            z = _ssm_scan(u, *sw, ssm_d[a], batch=batch)
            h = _matmul_norm_res(z, _to_bf16(ssm_w_glu, a), h, mix_post_g[l], bm=512, glu=True, name="ssm_glu_out")
        else:
            b = l - n_a
            lambda_init = 0.8 - 0.6 * math.exp(-0.3 * l)
            q = _norm_matmul(h, mix_pre_g[l], _to_bf16(attn_w_q, b), out_dtype=BF16, bm=1024, name="q_proj",
                             rope=(cos, sin), rope_cols=d, scale=hd ** -0.5)
            lam_vecs = jnp.zeros((8, hd), F32).at[0:4].set(
                jnp.stack([lam_q1[b], lam_k1[b], lam_q2[b], lam_k2[b]]).astype(F32))
            o = _diff_attention(q, kv, lam_vecs, attn_subln_g[b], batch=batch, heads=heads, hd=hd,
                                lambda_init=lambda_init, blk=512)
            h = _matmul_norm_res(o, _to_bf16(attn_w_o, b), h, mix_post_g[l], bm=1024, name="attn_out_proj")
        h = _mlp(h, mlp_pre_g[l], _to_bf16(mlp_w_up, l), _to_bf16(mlp_w_down, l), mlp_post_g[l],
                 bm=1024, bf=1024)
        if l == n_a - 1:
            kv = _norm_matmul(h, kv_norm_g, _to_bf16(w_kv), out_dtype=BF16, bm=512, name="kv_proj",
                              rope=(cos, sin), rope_cols=d, scale=1.0)
    return h.reshape(batch, seq, d)
```

```python
import functools
import math

import jax
import jax.numpy as jnp
from jax import lax
from jax.experimental import pallas as pl
from jax.experimental.pallas import tpu as pltpu

EPS = 1e-6
ROPE_THETA = 10000.0
LANES = 128
SSM_TC = 8
SSM_SEQS = 4
VMEM_LIMIT_BYTES = 56 * 1024 * 1024
MLP_VMEM_LIMIT_BYTES = 58 * 1024 * 1024
MASK_VALUE = -1e30
MLP_DOWN_COLS = 512
ROW_SUB = 256
COL_SLAB = 512
CAST_BLOCK_ELEMS = 2 * 1024 * 1024

F32 = jnp.float32
BF16 = jnp.bfloat16


def _params(*sem, vmem_limit_bytes=VMEM_LIMIT_BYTES):
    return pltpu.CompilerParams(dimension_semantics=sem, vmem_limit_bytes=vmem_limit_bytes)


def _rms(x, g):
    return x * lax.rsqrt(jnp.mean(x * x, axis=-1, keepdims=True) + EPS) * g


def _dot(a, b):
    return jnp.dot(a, b, preferred_element_type=F32)


def _rope(t, cos, sin_signed):
    return t * cos + pltpu.roll(t, LANES // 2, 1) * sin_signed


def _make_norm_matmul_kernel(rope_cols, scale):
    def kernel(x_ref, g_ref, w_ref, *rest):
        if rope_cols:
            cos_ref, sin_ref, o_ref = rest
        else:
            (o_ref,) = rest
        bm, n = o_ref.shape
        sub, slab = min(ROW_SUB, bm), min(COL_SLAB, n)
        g = g_ref[...]
        for r in range(bm // sub):
            rs = slice(r * sub, (r + 1) * sub)
            xn = _rms(x_ref[rs, :], g).astype(BF16)
            for c in range(n // slab):
                acc = _dot(xn, w_ref[:, c * slab:(c + 1) * slab])
                if c * slab < rope_cols:
                    cos, sin = cos_ref[rs, :], sin_ref[rs, :]
                    for cc in range(slab // LANES):
                        sl = slice(cc * LANES, (cc + 1) * LANES)
                        osl = slice(c * slab + cc * LANES, c * slab + (cc + 1) * LANES)
                        o_ref[rs, osl] = (_rope(acc[:, sl], cos, sin) * scale).astype(o_ref.dtype)
                else:
                    o_ref[rs, c * slab:(c + 1) * slab] = acc.astype(o_ref.dtype)

    return kernel


def _norm_matmul(x, g, w, *, out_dtype, bm, name, rope=None, rope_cols=0, scale=1.0):
    m, k = x.shape
    n = w.shape[1]
    bm = min(bm, m)
    args = [x, g.reshape(1, k), w]
    if rope is not None:
        cos, sin = rope
        seq = cos.shape[0]
        bm = min(bm, seq)
        assert rope_cols % min(COL_SLAB, n) == 0 and seq % bm == 0
        nseq = seq // bm
        args += [cos, sin]
    in_specs = [
        pl.BlockSpec((bm, k), lambda i: (i, 0)),
        pl.BlockSpec((1, k), lambda i: (0, 0)),
        pl.BlockSpec((k, n), lambda i: (0, 0), pipeline_mode=pl.Buffered(1)),
    ]
    if rope is not None:
        in_specs += [pl.BlockSpec((bm, LANES), lambda i: (i % nseq, 0))] * 2
    return pl.pallas_call(
        _make_norm_matmul_kernel(rope_cols, scale),
        grid=(m // bm,),
        in_specs=in_specs,
        out_specs=pl.BlockSpec((bm, n), lambda i: (i, 0)),
        out_shape=jax.ShapeDtypeStruct((m, n), out_dtype),
        compiler_params=_params("parallel"),
        name=name,
    )(*args)


def _ssm_prep_kernel(pw_ref, bb_ref, cc_ref, wi_ref, wn_ref, wo_ref, *, tc, p_ch, n_st):
    s_dim = pw_ref.shape[1]
    gpb = LANES // p_ch
    row_g = lax.shift_right_logical(lax.broadcasted_iota(jnp.int32, (LANES, s_dim), 0), int(math.log2(p_ch)))
    col_g = lax.shift_right_logical(lax.broadcasted_iota(jnp.int32, (LANES, s_dim), 1), int(math.log2(n_st)))
    same_group = row_g == col_g

    def expand(x):
        return jnp.where(same_group, jnp.concatenate([x] * gpb, axis=0), 0.0)

    b_r, b_i = expand(bb_ref[0]), expand(bb_ref[1])
    c_r, c_i = expand(cc_ref[0]), expand(cc_ref[1])
    c0 = jnp.concatenate([c_r, -c_i], axis=1).T
    kd = []
    for d in range(tc):
        pr, pi = pw_ref[d:d + 1, :], pw_ref[tc + 1 + d:tc + 2 + d, :]
        e = jnp.concatenate([pr * b_r - pi * b_i, pr * b_i + pi * b_r], axis=1)
        wn_ref[(tc - 1 - d) * LANES:(tc - d) * LANES, :] = e.astype(BF16)
        kd.append(jnp.dot(e, c0, precision=lax.Precision.HIGHEST,
                          preferred_element_type=F32).astype(BF16))
    for t in range(tc):
        pr, pi = pw_ref[t + 1:t + 2, :], pw_ref[tc + 2 + t:tc + 3 + t, :]
        e = jnp.concatenate([pr * c_r - pi * c_i, -(pr * c_i + pi * c_r)], axis=1)
        wo_ref[:, t * LANES:(t + 1) * LANES] = e.T.astype(BF16)
    zero = jnp.zeros((LANES, LANES), BF16)
    for s in range(tc):
        for t in range(tc):
            wi_ref[s * LANES:(s + 1) * LANES, t * LANES:(t + 1) * LANES] = kd[t - s] if t >= s else zero


def _ssm_weights(a_re, a_im, log_dt, b_re, b_im, c_re, c_im, tc):
    g_n, n_st = a_re.shape
    p_ch = b_re.shape[-1]
    gpb = LANES // p_ch
    nj = g_n // gpb
    s_dim = gpb * n_st
    assert p_ch & (p_ch - 1) == 0 and n_st & (n_st - 1) == 0
    step = jnp.exp(log_dt.astype(F32))[:, None]
    lam_re = jnp.minimum(a_re.astype(F32), -1e-4)
    lam_im = a_im.astype(F32)
    mag = jnp.exp(step * lam_re)
    abar_re = mag * jnp.cos(step * lam_im)
    abar_im = mag * jnp.sin(step * lam_im)
    den = lam_re * lam_re + lam_im * lam_im
    nr = abar_re - 1.0
    ni = abar_im
    coef_re = (nr * lam_re + ni * lam_im) / den
    coef_im = (ni * lam_re - nr * lam_im) / den
    bre, bim = b_re.astype(F32), b_im.astype(F32)
    bbar_re = coef_re[..., None] * bre - coef_im[..., None] * bim
    bbar_im = coef_re[..., None] * bim + coef_im[..., None] * bre
    dd = jnp.arange(tc + 1, dtype=F32)[:, None, None]
    pmag = jnp.exp(dd * (step * lam_re))
    pw = jnp.concatenate([pmag * jnp.cos(dd * (step * lam_im)),
                          pmag * jnp.sin(dd * (step * lam_im))], axis=0)
    pw = pw.reshape(2 * (tc + 1), nj, s_dim).transpose(1, 0, 2)
    bb = jnp.stack([bbar_re, bbar_im]).reshape(2, nj, gpb, n_st, p_ch)
    bb = bb.transpose(1, 0, 4, 2, 3).reshape(nj, 2, p_ch, s_dim)
    cc = jnp.stack([c_re.astype(F32), c_im.astype(F32)]).reshape(2, nj, gpb, p_ch, n_st)
    cc = cc.transpose(1, 0, 3, 2, 4).reshape(nj, 2, p_ch, s_dim)
    k_dim = tc * LANES
    wshape = jax.ShapeDtypeStruct((nj, k_dim, k_dim), BF16)
    assert 2 * s_dim == k_dim
    w_intra, w_in, w_out = pl.pallas_call(
        functools.partial(_ssm_prep_kernel, tc=tc, p_ch=p_ch, n_st=n_st),
        grid=(nj,),
        in_specs=[
            pl.BlockSpec((None, 2 * (tc + 1), s_dim), lambda j: (j, 0, 0)),
            pl.BlockSpec((None, 2, p_ch, s_dim), lambda j: (j, 0, 0, 0)),
            pl.BlockSpec((None, 2, p_ch, s_dim), lambda j: (j, 0, 0, 0)),
        ],
        out_specs=[pl.BlockSpec((None, k_dim, k_dim), lambda j: (j, 0, 0))] * 3,
        out_shape=[wshape] * 3,
        compiler_params=_params("parallel"),
        name="ssm_prep",
    )(pw, bb, cc)
    a_step = jnp.stack([pw[:, tc], pw[:, 2 * tc + 1]], axis=1)
    return w_intra, w_in, w_out, a_step


def _gelu_tanh(x):
    return 0.5 * x * (1.0 + jnp.tanh(math.sqrt(2.0 / math.pi) * (x + 0.044715 * (x * x * x))))


def _ssm_kernel(u_ref, wi_ref, wn_ref, wo_ref, a_ref, d_ref, z_ref, st_ref, zs_ref, *, nb, nc):
    tc = SSM_TC
    rows = nb * nc
    nk = st_ref.shape[0] // 2

    def step_rows(s):
        return pl.ds(s, rows, stride=tc)

    lhs = jnp.concatenate([u_ref[step_rows(s), :].astype(BF16) for s in range(tc)], axis=1)
    contrib = _dot(lhs, wn_ref[...])
    for k in range(2 * nk):
        for b in range(nb):
            st_ref[k, pl.ds(b, nc, stride=nb), :] = contrib[b * nc:(b + 1) * nc, k * LANES:(k + 1) * LANES]

    a = a_ref[...]
    ar = [jnp.broadcast_to(a[0:1, k * LANES:(k + 1) * LANES], (nb, LANES)) for k in range(nk)]
    ai = [jnp.broadcast_to(a[1:2, k * LANES:(k + 1) * LANES], (nb, LANES)) for k in range(nk)]

    def step(c, carry):
        rws = pl.ds(pl.multiple_of(c * nb, nb), nb)
        out = []
        for k in range(nk):
            sr, si = carry[2 * k], carry[2 * k + 1]
            cr = st_ref[k, rws, :]
            ci = st_ref[nk + k, rws, :]
            st_ref[k, rws, :] = sr
            st_ref[nk + k, rws, :] = si
            out += [ar[k] * sr - ai[k] * si + cr, ar[k] * si + ai[k] * sr + ci]
        return tuple(out)

    y_intra = _dot(lhs, wi_ref[...])
    zero = jnp.zeros((nb, LANES), F32)
    lax.fori_loop(0, nc, step, (zero,) * (2 * nk), unroll=True)

    state = jnp.concatenate(
        [jnp.concatenate([st_ref[k, pl.ds(b, nc, stride=nb), :] for b in range(nb)], axis=0).astype(BF16)
         for k in range(2 * nk)], axis=1)
    y = y_intra + _dot(state, wo_ref[...])
    d = d_ref[...]
    for t in range(tc):
        yt = y[:, t * LANES:(t + 1) * LANES] + d * u_ref[step_rows(t), :]
        zs_ref[step_rows(t), :] = _gelu_tanh(yt)
    z_ref[...] = zs_ref[...].astype(z_ref.dtype)


def _ssm_scan(u, w_intra, w_in, w_out, a_step, d_skip, *, batch):
    t, d = u.shape
    seq = t // batch
    tc = SSM_TC
    nc = seq // tc
    nb = min(batch, SSM_SEQS)
    k_dim = tc * LANES
    return pl.pallas_call(
        functools.partial(_ssm_kernel, nb=nb, nc=nc),
        grid=(d // LANES, batch // nb),
        in_specs=[
            pl.BlockSpec((nb * seq, LANES), lambda j, b: (b, j)),
            pl.BlockSpec((None, k_dim, k_dim), lambda j, b: (j, 0, 0)),
            pl.BlockSpec((None, k_dim, k_dim), lambda j, b: (j, 0, 0)),
            pl.BlockSpec((None, k_dim, k_dim), lambda j, b: (j, 0, 0)),
            pl.BlockSpec((None, 2, k_dim // 2), lambda j, b: (j, 0, 0)),
            pl.BlockSpec((1, LANES), lambda j, b: (0, j)),
        ],
        out_specs=pl.BlockSpec((nb * seq, LANES), lambda j, b: (b, j)),
        out_shape=jax.ShapeDtypeStruct((t, d), BF16),
        scratch_shapes=[pltpu.VMEM((k_dim // LANES, nc * nb, LANES), F32),
                        pltpu.VMEM((nb * seq, LANES), F32)],
        compiler_params=_params("parallel", "parallel"),
        name="ssm_scan",
    )(u, w_intra, w_in, w_out, a_step, d_skip.reshape(1, d))


def _make_matmul_norm_res_kernel(glu):
    def kernel(a_ref, w_ref, x_ref, g_ref, o_ref):
        bm, d = o_ref.shape
        sub, slab = min(ROW_SUB, bm), min(COL_SLAB, d)
        g = g_ref[...]
        for r in range(bm // sub):
            rs = slice(r * sub, (r + 1) * sub)
            a = a_ref[rs, :]
            ssq = jnp.zeros((sub, 1), F32)
            for c in range(d // slab):
                cs = slice(c * slab, (c + 1) * slab)
                mix = _dot(a, w_ref[:, cs])
                if glu:
                    gate = _dot(a, w_ref[:, d + c * slab:d + (c + 1) * slab])
                    mix = mix * (1.0 / (1.0 + jnp.exp(-gate)))
                o_ref[rs, cs] = mix
                ssq = ssq + jnp.sum(mix * mix, axis=1, keepdims=True)
            o_ref[rs, :] = x_ref[rs, :] + o_ref[rs, :] * lax.rsqrt(ssq * (1.0 / d) + EPS) * g

    return kernel


def _matmul_norm_res(a, w, x, g, *, bm, name, glu=False):
    m, k = a.shape
    d = x.shape[1]
    bm = min(bm, m)
    return pl.pallas_call(
        _make_matmul_norm_res_kernel(glu),
        grid=(m // bm,),
        in_specs=[
            pl.BlockSpec((bm, k), lambda i: (i, 0)),
            pl.BlockSpec(w.shape, lambda i: (0, 0), pipeline_mode=pl.Buffered(1)),
            pl.BlockSpec((bm, d), lambda i: (i, 0)),
            pl.BlockSpec((1, d), lambda i: (0, 0)),
        ],
        out_specs=pl.BlockSpec((bm, d), lambda i: (i, 0)),
        out_shape=jax.ShapeDtypeStruct((m, d), F32),
        compiler_params=_params("parallel"),
        name=name,
    )(a, w, x, g.reshape(1, d))


def _mlp_kernel(x_ref, gpre_ref, wu_ref, wd_ref, gpost_ref, o_ref, xn_ref, *, nf):
    f = pl.program_id(1)
    bm, d = o_ref.shape
    bn = min(MLP_DOWN_COLS, d)

    def up_down(xn, rs, first):
        a = jnp.square(jnp.maximum(_dot(xn, wu_ref[...]), 0.0)).astype(BF16)
        for c in range(d // bn):
            sl = slice(c * bn, (c + 1) * bn)
            part = _dot(a, wd_ref[:, sl])
            if first:
                o_ref[rs, sl] = part
            else:
                o_ref[rs, sl] += part

    def edge_step(first, last):
        sub = min(ROW_SUB, bm)
        for r in range(bm // sub):
            rs = slice(r * sub, (r + 1) * sub)
            if first:
                xn = _rms(x_ref[rs, :], gpre_ref[...]).astype(BF16)
                xn_ref[rs, :] = xn
            else:
                xn = xn_ref[rs, :]
            up_down(xn, rs, first)
            if last:
                o_ref[rs, :] = x_ref[rs, :] + _rms(o_ref[rs, :], gpost_ref[...])

    @pl.when(f == 0)
    def _():
        edge_step(True, nf == 1)

    if nf > 2:
        @pl.when(jnp.logical_and(f > 0, f < nf - 1))
        def _():
            up_down(xn_ref[...], slice(None), False)

    if nf > 1:
        @pl.when(f == nf - 1)
        def _():
            edge_step(False, True)


def _mlp(x, g_pre, w_up, w_down, g_post, *, bm, bf):
    m, d = x.shape
    dff = w_up.shape[1]
    bm, bf = min(bm, m), min(bf, dff)
    return pl.pallas_call(
        functools.partial(_mlp_kernel, nf=dff // bf),
        grid=(m // bm, dff // bf),
        in_specs=[
            pl.BlockSpec((bm, d), lambda i, f: (i, 0), pipeline_mode=pl.Buffered(1)),
            pl.BlockSpec((1, d), lambda i, f: (0, 0)),
            pl.BlockSpec((d, bf), lambda i, f: (0, f)),
            pl.BlockSpec((bf, d), lambda i, f: (f, 0)),
            pl.BlockSpec((1, d), lambda i, f: (0, 0)),
        ],
        out_specs=pl.BlockSpec((bm, d), lambda i, f: (i, 0)),
        out_shape=jax.ShapeDtypeStruct((m, d), F32),
        scratch_shapes=[pltpu.VMEM((bm, d), BF16)],
        compiler_params=_params("parallel", "arbitrary", vmem_limit_bytes=MLP_VMEM_LIMIT_BYTES),
        name="mlp",
    )(x, g_pre.reshape(1, d), w_up, w_down, g_post.reshape(1, d))


def _attn_kernel(q_ref, k_ref, v_ref, lam_ref, g_ref, o_ref, *, hd, lambda_init):
    seq = q_ref.shape[0]
    sub = min(ROW_SUB, seq)
    nt = (((1,), (1,)), ((), ()))
    lv = lam_ref[...]
    lam = (jnp.exp(jnp.sum(lv[0:1] * lv[1:2], axis=1, keepdims=True))
           - jnp.exp(jnp.sum(lv[2:3] * lv[3:4], axis=1, keepdims=True)) + lambda_init)
    row = lax.broadcasted_iota(jnp.int32, (sub, sub), 0)
    col = lax.broadcasted_iota(jnp.int32, (sub, sub), 1)
    causal = col <= row

    def softmax_v(rs, c0, kv_len):
        s = lax.dot_general(q_ref[rs, c0:c0 + hd], k_ref[0:kv_len, c0:c0 + hd], nt,
                            preferred_element_type=F32)
        s_diag = jnp.where(causal, s[:, kv_len - sub:], MASK_VALUE)
        s = s_diag if kv_len == sub else jnp.concatenate([s[:, :kv_len - sub], s_diag], axis=1)
        p = jnp.exp(s - jnp.max(s, axis=1, keepdims=True))
        l = jnp.sum(p, axis=1, keepdims=True)
        return _dot(p.astype(BF16), v_ref[0:kv_len, :]) * (1.0 / l)

    for r0 in reversed(range(0, seq, sub)):
        rs = slice(r0, r0 + sub)
        o = softmax_v(rs, 0, r0 + sub) - lam * softmax_v(rs, hd, r0 + sub)
        o_ref[rs, :] = (_rms(o, g_ref[...]) * (1.0 - lambda_init)).astype(o_ref.dtype)


def _diff_attention(q, kv, lam_vecs, g_sub, *, batch, heads, hd, lambda_init):
    t, d = q.shape
    seq = t // batch
    head_block = pl.BlockSpec((seq, 2 * hd), lambda b, h: (b, h))
    return pl.pallas_call(
        functools.partial(_attn_kernel, hd=hd, lambda_init=lambda_init),
        grid=(batch, heads),
        in_specs=[
            head_block,
            head_block,
            pl.BlockSpec((seq, 2 * hd), lambda b, h: (b, heads + h)),
            pl.BlockSpec((8, hd), lambda b, h: (0, 0)),
            pl.BlockSpec((1, 2 * hd), lambda b, h: (0, 0)),
        ],
        out_specs=head_block,
        out_shape=jax.ShapeDtypeStruct((t, d), BF16),
        compiler_params=_params("parallel", "parallel"),
        name="diff_attn",
    )(q, kv, kv, lam_vecs, g_sub.reshape(1, 2 * hd))


def _cast_kernel(w_ref, o_ref):
    o_ref[...] = w_ref[...].astype(o_ref.dtype)


def _to_bf16(w, layer=None):
    if layer is None:
        w, layer = w[None], 0
    _, k, n = w.shape
    rows = max(8, min(k, CAST_BLOCK_ELEMS // n))
    assert k % rows == 0
    return pl.pallas_call(
        _cast_kernel,
        grid=(k // rows,),
        in_specs=[pl.BlockSpec((None, rows, n), lambda i: (layer, i, 0))],
        out_specs=pl.BlockSpec((rows, n), lambda i: (i, 0)),
        out_shape=jax.ShapeDtypeStruct((k, n), BF16),
        compiler_params=_params("parallel"),
        name="cast_bf16",
    )(w)


def _rope_tables(seq, hd):
    pos = jnp.arange(seq, dtype=F32)
    inv_freq = 1.0 / (ROPE_THETA ** (jnp.arange(0, hd, 2, dtype=F32) / hd))
    ang = pos[:, None] * inv_freq[None, :]
    emb = jnp.concatenate([ang, ang], axis=-1)
    sign = jnp.where(jnp.arange(hd) < hd // 2, -1.0, 1.0).astype(F32)
    return jnp.cos(emb), jnp.sin(emb) * sign


def kernel(x, mix_pre_g, mix_post_g, mlp_pre_g, mlp_post_g, ssm_w_in, ssm_a_re, ssm_a_im, ssm_log_dt, ssm_b_re, ssm_b_im, ssm_c_re, ssm_c_im, ssm_d, ssm_w_glu, kv_norm_g, w_kv, attn_w_q, lam_q1, lam_k1, lam_q2, lam_k2, attn_subln_g, attn_w_o, mlp_w_up, mlp_w_down):
    batch, seq, d = x.shape
    t = batch * seq
    depth = mix_pre_g.shape[0]
    n_a = ssm_w_in.shape[0]
    hd = lam_q1.shape[1]
    heads = d // (2 * hd)
    cos, sin = _rope_tables(seq, hd)

    h = x.reshape(t, d)
    kv = None
    for l in range(depth):
        if l < n_a:
            a = l
            u = _norm_matmul(h, mix_pre_g[l], _to_bf16(ssm_w_in, a), out_dtype=F32, bm=1024, name="ssm_in_proj")
            sw = _ssm_weights(ssm_a_re[a], ssm_a_im[a], ssm_log_dt[a], ssm_b_re[a], ssm_b_im[a],
                              ssm_c_re[a], ssm_c_im[a], SSM_TC)
            z = _ssm_scan(u, *sw, ssm_d[a], batch=batch)
            h = _matmul_norm_res(z, _to_bf16(ssm_w_glu, a), h, mix_post_g[l], bm=512, glu=True, name="ssm_glu_out")
        else:
            b = l - n_a
            lambda_init = 0.8 - 0.6 * math.exp(-0.3 * l)
            q = _norm_matmul(h, mix_pre_g[l], _to_bf16(attn_w_q, b), out_dtype=BF16, bm=1024, name="q_proj",
                             rope=(cos, sin), rope_cols=d, scale=hd ** -0.5)
            lam_vecs = jnp.zeros((8, hd), F32).at[0:4].set(
                jnp.stack([lam_q1[b], lam_k1[b], lam_q2[b], lam_k2[b]]).astype(F32))
            o = _diff_attention(q, kv, lam_vecs, attn_subln_g[b], batch=batch, heads=heads, hd=hd,
                                lambda_init=lambda_init)
            h = _matmul_norm_res(o, _to_bf16(attn_w_o, b), h, mix_post_g[l], bm=1024, name="attn_out_proj")
        h = _mlp(h, mlp_pre_g[l], _to_bf16(mlp_w_up, l), _to_bf16(mlp_w_down, l), mlp_post_g[l],
                 bm=1024, bf=1024)
        if l == n_a - 1:
            kv = _norm_matmul(h, kv_norm_g, _to_bf16(w_kv), out_dtype=BF16, bm=512, name="kv_proj",
                              rope=(cos, sin), rope_cols=d, scale=1.0)
    return h.reshape(batch, seq, d)
```

```python
import functools
import math

import jax
import jax.numpy as jnp
from jax import lax
from jax.experimental import pallas as pl
from jax.experimental.pallas import tpu as pltpu

EPS = 1e-6
ROPE_THETA = 10000.0
LANES = 128
SSM_TC = 8
SSM_SEQS = 4
VMEM_LIMIT_BYTES = 56 * 1024 * 1024
MLP_VMEM_LIMIT_BYTES = 58 * 1024 * 1024
MASK_VALUE = -1e30
MLP_DOWN_COLS = 512
ROW_SUB = 256
COL_SLAB = 512
CAST_BLOCK_ELEMS = 2 * 1024 * 1024
ATTN_SOFTMAX_LAG = 1
ATTN_PV_LAG = 2

F32 = jnp.float32
BF16 = jnp.bfloat16


def _params(*sem, vmem_limit_bytes=VMEM_LIMIT_BYTES):
    return pltpu.CompilerParams(dimension_semantics=sem, vmem_limit_bytes=vmem_limit_bytes)


def _rms(x, g):
    return x * lax.rsqrt(jnp.mean(x * x, axis=-1, keepdims=True) + EPS) * g


def _dot(a, b):
    return jnp.dot(a, b, preferred_element_type=F32)


def _rope(t, cos, sin_signed):
    return t * cos + pltpu.roll(t, LANES // 2, 1) * sin_signed


def _make_norm_matmul_kernel(rope_cols, scale):
    def kernel(x_ref, g_ref, w_ref, *rest):
        if rope_cols:
            cos_ref, sin_ref, o_ref = rest
        else:
            (o_ref,) = rest
        bm, n = o_ref.shape
        sub, slab = min(ROW_SUB, bm), min(COL_SLAB, n)
        g = g_ref[...]
        for r in range(bm // sub):
            rs = slice(r * sub, (r + 1) * sub)
            xn = _rms(x_ref[rs, :], g).astype(BF16)
            for c in range(n // slab):
                acc = _dot(xn, w_ref[:, c * slab:(c + 1) * slab])
                if c * slab < rope_cols:
                    cos, sin = cos_ref[rs, :], sin_ref[rs, :]
                    for cc in range(slab // LANES):
                        sl = slice(cc * LANES, (cc + 1) * LANES)
                        osl = slice(c * slab + cc * LANES, c * slab + (cc + 1) * LANES)
                        o_ref[rs, osl] = (_rope(acc[:, sl], cos, sin) * scale).astype(o_ref.dtype)
                else:
                    o_ref[rs, c * slab:(c + 1) * slab] = acc.astype(o_ref.dtype)

    return kernel


def _norm_matmul(x, g, w, *, out_dtype, bm, name, rope=None, rope_cols=0, scale=1.0):
    m, k = x.shape
    n = w.shape[1]
    bm = min(bm, m)
    args = [x, g.reshape(1, k), w]
    if rope is not None:
        cos, sin = rope
        seq = cos.shape[0]
        bm = min(bm, seq)
        assert rope_cols % min(COL_SLAB, n) == 0 and seq % bm == 0
        nseq = seq // bm
        args += [cos, sin]
    in_specs = [
        pl.BlockSpec((bm, k), lambda i: (i, 0)),
        pl.BlockSpec((1, k), lambda i: (0, 0)),
        pl.BlockSpec((k, n), lambda i: (0, 0), pipeline_mode=pl.Buffered(1)),
    ]
    if rope is not None:
        in_specs += [pl.BlockSpec((bm, LANES), lambda i: (i % nseq, 0))] * 2
    return pl.pallas_call(
        _make_norm_matmul_kernel(rope_cols, scale),
        grid=(m // bm,),
        in_specs=in_specs,
        out_specs=pl.BlockSpec((bm, n), lambda i: (i, 0)),
        out_shape=jax.ShapeDtypeStruct((m, n), out_dtype),
        compiler_params=_params("parallel"),
        name=name,
    )(*args)


def _ssm_prep_kernel(pw_ref, bb_ref, cc_ref, wi_ref, wn_ref, wo_ref, *, tc, p_ch, n_st):
    s_dim = pw_ref.shape[1]
    gpb = LANES // p_ch
    row_g = lax.shift_right_logical(lax.broadcasted_iota(jnp.int32, (LANES, s_dim), 0), int(math.log2(p_ch)))
    col_g = lax.shift_right_logical(lax.broadcasted_iota(jnp.int32, (LANES, s_dim), 1), int(math.log2(n_st)))
    same_group = row_g == col_g

    def expand(x):
        return jnp.where(same_group, jnp.concatenate([x] * gpb, axis=0), 0.0)

    b_r, b_i = expand(bb_ref[0]), expand(bb_ref[1])
    c_r, c_i = expand(cc_ref[0]), expand(cc_ref[1])
    c0 = jnp.concatenate([c_r, -c_i], axis=1).T
    kd = []
    for d in range(tc):
        pr, pi = pw_ref[d:d + 1, :], pw_ref[tc + 1 + d:tc + 2 + d, :]
        e = jnp.concatenate([pr * b_r - pi * b_i, pr * b_i + pi * b_r], axis=1)
        wn_ref[(tc - 1 - d) * LANES:(tc - d) * LANES, :] = e.astype(BF16)
        kd.append(jnp.dot(e, c0, precision=lax.Precision.HIGHEST,
                          preferred_element_type=F32).astype(BF16))
    for t in range(tc):
        pr, pi = pw_ref[t + 1:t + 2, :], pw_ref[tc + 2 + t:tc + 3 + t, :]
        e = jnp.concatenate([pr * c_r - pi * c_i, -(pr * c_i + pi * c_r)], axis=1)
        wo_ref[:, t * LANES:(t + 1) * LANES] = e.T.astype(BF16)
    zero = jnp.zeros((LANES, LANES), BF16)
    for s in range(tc):
        for t in range(tc):
            wi_ref[s * LANES:(s + 1) * LANES, t * LANES:(t + 1) * LANES] = kd[t - s] if t >= s else zero


def _ssm_weights(a_re, a_im, log_dt, b_re, b_im, c_re, c_im, tc):
    g_n, n_st = a_re.shape
    p_ch = b_re.shape[-1]
    gpb = LANES // p_ch
    nj = g_n // gpb
    s_dim = gpb * n_st
    assert p_ch & (p_ch - 1) == 0 and n_st & (n_st - 1) == 0
    step = jnp.exp(log_dt.astype(F32))[:, None]
    lam_re = jnp.minimum(a_re.astype(F32), -1e-4)
    lam_im = a_im.astype(F32)
    mag = jnp.exp(step * lam_re)
    abar_re = mag * jnp.cos(step * lam_im)
    abar_im = mag * jnp.sin(step * lam_im)
    den = lam_re * lam_re + lam_im * lam_im
    nr = abar_re - 1.0
    ni = abar_im
    coef_re = (nr * lam_re + ni * lam_im) / den
    coef_im = (ni * lam_re - nr * lam_im) / den
    bre, bim = b_re.astype(F32), b_im.astype(F32)
    bbar_re = coef_re[..., None] * bre - coef_im[..., None] * bim
    bbar_im = coef_re[..., None] * bim + coef_im[..., None] * bre
    dd = jnp.arange(tc + 1, dtype=F32)[:, None, None]
    pmag = jnp.exp(dd * (step * lam_re))
    pw = jnp.concatenate([pmag * jnp.cos(dd * (step * lam_im)),
                          pmag * jnp.sin(dd * (step * lam_im))], axis=0)
    pw = pw.reshape(2 * (tc + 1), nj, s_dim).transpose(1, 0, 2)
    bb = jnp.stack([bbar_re, bbar_im]).reshape(2, nj, gpb, n_st, p_ch)
    bb = bb.transpose(1, 0, 4, 2, 3).reshape(nj, 2, p_ch, s_dim)
    cc = jnp.stack([c_re.astype(F32), c_im.astype(F32)]).reshape(2, nj, gpb, p_ch, n_st)
    cc = cc.transpose(1, 0, 3, 2, 4).reshape(nj, 2, p_ch, s_dim)
    k_dim = tc * LANES
    wshape = jax.ShapeDtypeStruct((nj, k_dim, k_dim), BF16)
    assert 2 * s_dim == k_dim
    w_intra, w_in, w_out = pl.pallas_call(
        functools.partial(_ssm_prep_kernel, tc=tc, p_ch=p_ch, n_st=n_st),
        grid=(nj,),
        in_specs=[
            pl.BlockSpec((None, 2 * (tc + 1), s_dim), lambda j: (j, 0, 0)),
            pl.BlockSpec((None, 2, p_ch, s_dim), lambda j: (j, 0, 0, 0)),
            pl.BlockSpec((None, 2, p_ch, s_dim), lambda j: (j, 0, 0, 0)),
        ],
        out_specs=[pl.BlockSpec((None, k_dim, k_dim), lambda j: (j, 0, 0))] * 3,
        out_shape=[wshape] * 3,
        compiler_params=_params("parallel"),
        name="ssm_prep",
    )(pw, bb, cc)
    a_step = jnp.stack([pw[:, tc], pw[:, 2 * tc + 1]], axis=1)
    return w_intra, w_in, w_out, a_step


def _gelu_tanh(x):
    return 0.5 * x * (1.0 + jnp.tanh(math.sqrt(2.0 / math.pi) * (x + 0.044715 * (x * x * x))))


def _ssm_kernel(u_ref, wi_ref, wn_ref, wo_ref, a_ref, d_ref, z_ref, st_ref, zs_ref, *, nb, nc):
    tc = SSM_TC
    rows = nb * nc
    nk = st_ref.shape[0] // 2

    def step_rows(s):
        return pl.ds(s, rows, stride=tc)

    lhs = jnp.concatenate([u_ref[step_rows(s), :].astype(BF16) for s in range(tc)], axis=1)
    contrib = _dot(lhs, wn_ref[...])
    for k in range(2 * nk):
        for b in range(nb):
            st_ref[k, pl.ds(b, nc, stride=nb), :] = contrib[b * nc:(b + 1) * nc, k * LANES:(k + 1) * LANES]

    a = a_ref[...]
    ar = [jnp.broadcast_to(a[0:1, k * LANES:(k + 1) * LANES], (nb, LANES)) for k in range(nk)]
    ai = [jnp.broadcast_to(a[1:2, k * LANES:(k + 1) * LANES], (nb, LANES)) for k in range(nk)]

    def step(c, carry):
        rws = pl.ds(pl.multiple_of(c * nb, nb), nb)
        out = []
        for k in range(nk):
            sr, si = carry[2 * k], carry[2 * k + 1]
            cr = st_ref[k, rws, :]
            ci = st_ref[nk + k, rws, :]
            st_ref[k, rws, :] = sr
            st_ref[nk + k, rws, :] = si
            out += [ar[k] * sr - ai[k] * si + cr, ar[k] * si + ai[k] * sr + ci]
        return tuple(out)

    y_intra = _dot(lhs, wi_ref[...])
    zero = jnp.zeros((nb, LANES), F32)
    lax.fori_loop(0, nc, step, (zero,) * (2 * nk), unroll=True)

    state = jnp.concatenate(
        [jnp.concatenate([st_ref[k, pl.ds(b, nc, stride=nb), :] for b in range(nb)], axis=0).astype(BF16)
         for k in range(2 * nk)], axis=1)
    y = y_intra + _dot(state, wo_ref[...])
    d = d_ref[...]
    for t in range(tc):
        yt = y[:, t * LANES:(t + 1) * LANES] + d * u_ref[step_rows(t), :]
        zs_ref[step_rows(t), :] = _gelu_tanh(yt)
    z_ref[...] = zs_ref[...].astype(z_ref.dtype)


def _ssm_scan(u, w_intra, w_in, w_out, a_step, d_skip, *, batch):
    t, d = u.shape
    seq = t // batch
    tc = SSM_TC
    nc = seq // tc
    nb = min(batch, SSM_SEQS)
    k_dim = tc * LANES
    return pl.pallas_call(
        functools.partial(_ssm_kernel, nb=nb, nc=nc),
        grid=(d // LANES, batch // nb),
        in_specs=[
            pl.BlockSpec((nb * seq, LANES), lambda j, b: (b, j)),
            pl.BlockSpec((None, k_dim, k_dim), lambda j, b: (j, 0, 0)),
            pl.BlockSpec((None, k_dim, k_dim), lambda j, b: (j, 0, 0)),
            pl.BlockSpec((None, k_dim, k_dim), lambda j, b: (j, 0, 0)),
            pl.BlockSpec((None, 2, k_dim // 2), lambda j, b: (j, 0, 0)),
            pl.BlockSpec((1, LANES), lambda j, b: (0, j)),
        ],
        out_specs=pl.BlockSpec((nb * seq, LANES), lambda j, b: (b, j)),
        out_shape=jax.ShapeDtypeStruct((t, d), BF16),
        scratch_shapes=[pltpu.VMEM((k_dim // LANES, nc * nb, LANES), F32),
                        pltpu.VMEM((nb * seq, LANES), F32)],
        compiler_params=_params("parallel", "parallel"),
        name="ssm_scan",
    )(u, w_intra, w_in, w_out, a_step, d_skip.reshape(1, d))


def _make_matmul_norm_res_kernel(glu):
    def kernel(a_ref, w_ref, x_ref, g_ref, o_ref):
        bm, d = o_ref.shape
        sub, slab = min(ROW_SUB, bm), min(COL_SLAB, d)
        g = g_ref[...]
        for r in range(bm // sub):
            rs = slice(r * sub, (r + 1) * sub)
            a = a_ref[rs, :]
            ssq = jnp.zeros((sub, 1), F32)
            for c in range(d // slab):
                cs = slice(c * slab, (c + 1) * slab)
                mix = _dot(a, w_ref[:, cs])
                if glu:
                    gate = _dot(a, w_ref[:, d + c * slab:d + (c + 1) * slab])
                    mix = mix * (1.0 / (1.0 + jnp.exp(-gate)))
                o_ref[rs, cs] = mix
                ssq = ssq + jnp.sum(mix * mix, axis=1, keepdims=True)
            o_ref[rs, :] = x_ref[rs, :] + o_ref[rs, :] * lax.rsqrt(ssq * (1.0 / d) + EPS) * g

    return kernel


def _matmul_norm_res(a, w, x, g, *, bm, name, glu=False):
    m, k = a.shape
    d = x.shape[1]
    bm = min(bm, m)
    return pl.pallas_call(
        _make_matmul_norm_res_kernel(glu),
        grid=(m // bm,),
        in_specs=[
            pl.BlockSpec((bm, k), lambda i: (i, 0)),
            pl.BlockSpec(w.shape, lambda i: (0, 0), pipeline_mode=pl.Buffered(1)),
            pl.BlockSpec((bm, d), lambda i: (i, 0)),
            pl.BlockSpec((1, d), lambda i: (0, 0)),
        ],
        out_specs=pl.BlockSpec((bm, d), lambda i: (i, 0)),
        out_shape=jax.ShapeDtypeStruct((m, d), F32),
        compiler_params=_params("parallel"),
        name=name,
    )(a, w, x, g.reshape(1, d))


def _mlp_kernel(x_ref, gpre_ref, wu_ref, wd_ref, gpost_ref, o_ref, xn_ref, *, nf):
    f = pl.program_id(1)
    bm, d = o_ref.shape
    bn = min(MLP_DOWN_COLS, d)

    def up_down(xn, rs, first):
        a = jnp.square(jnp.maximum(_dot(xn, wu_ref[...]), 0.0)).astype(BF16)
        for c in range(d // bn):
            sl = slice(c * bn, (c + 1) * bn)
            part = _dot(a, wd_ref[:, sl])
            if first:
                o_ref[rs, sl] = part
            else:
                o_ref[rs, sl] += part

    def edge_step(first, last):
        sub = min(ROW_SUB, bm)
        for r in range(bm // sub):
            rs = slice(r * sub, (r + 1) * sub)
            if first:
                xn = _rms(x_ref[rs, :], gpre_ref[...]).astype(BF16)
                xn_ref[rs, :] = xn
            else:
                xn = xn_ref[rs, :]
            up_down(xn, rs, first)
            if last:
                o_ref[rs, :] = x_ref[rs, :] + _rms(o_ref[rs, :], gpost_ref[...])

    @pl.when(f == 0)
    def _():
        edge_step(True, nf == 1)

    if nf > 2:
        @pl.when(jnp.logical_and(f > 0, f < nf - 1))
        def _():
            up_down(xn_ref[...], slice(None), False)

    if nf > 1:
        @pl.when(f == nf - 1)
        def _():
            edge_step(False, True)


def _mlp(x, g_pre, w_up, w_down, g_post, *, bm, bf):
    m, d = x.shape
    dff = w_up.shape[1]
    bm, bf = min(bm, m), min(bf, dff)
    return pl.pallas_call(
        functools.partial(_mlp_kernel, nf=dff // bf),
        grid=(m // bm, dff // bf),
        in_specs=[
            pl.BlockSpec((bm, d), lambda i, f: (i, 0), pipeline_mode=pl.Buffered(1)),
            pl.BlockSpec((1, d), lambda i, f: (0, 0)),
            pl.BlockSpec((d, bf), lambda i, f: (0, f)),
            pl.BlockSpec((bf, d), lambda i, f: (f, 0)),
            pl.BlockSpec((1, d), lambda i, f: (0, 0)),
        ],
        out_specs=pl.BlockSpec((bm, d), lambda i, f: (i, 0)),
        out_shape=jax.ShapeDtypeStruct((m, d), F32),
        scratch_shapes=[pltpu.VMEM((bm, d), BF16)],
        compiler_params=_params("parallel", "arbitrary", vmem_limit_bytes=MLP_VMEM_LIMIT_BYTES),
        name="mlp",
    )(x, g_pre.reshape(1, d), w_up, w_down, g_post.reshape(1, d))


def _attn_kernel(q_ref, k_ref, v_ref, lam_ref, g_ref, o_ref, *, hd, lambda_init):
    seq = q_ref.shape[0]
    sub = min(ROW_SUB, seq)
    nt = (((1,), (1,)), ((), ()))
    lv = lam_ref[...]
    lam = (jnp.exp(jnp.sum(lv[0:1] * lv[1:2], axis=1, keepdims=True))
           - jnp.exp(jnp.sum(lv[2:3] * lv[3:4], axis=1, keepdims=True)) + lambda_init)
    row = lax.broadcasted_iota(jnp.int32, (sub, sub), 0)
    col = lax.broadcasted_iota(jnp.int32, (sub, sub), 1)
    causal = col <= row

    chains = [(r0, c0) for r0 in reversed(range(0, seq, sub)) for c0 in (0, hd)]

    def scores(i):
        r0, c0 = chains[i]
        kv_len = r0 + sub
        s = lax.dot_general(q_ref[r0:r0 + sub, c0:c0 + hd], k_ref[0:kv_len, c0:c0 + hd], nt,
                            preferred_element_type=F32)
        s_diag = jnp.where(causal, s[:, kv_len - sub:], MASK_VALUE)
        return s_diag if kv_len == sub else jnp.concatenate([s[:, :kv_len - sub], s_diag], axis=1)

    def softmax(s):
        p = jnp.exp(s - jnp.max(s, axis=1, keepdims=True))
        return p.astype(BF16), jnp.sum(p, axis=1, keepdims=True)

    def weighted_v(i, p, l):
        return _dot(p, v_ref[0:chains[i][0] + sub, :]) * (1.0 / l)

    s_vals, p_vals, o_vals = {}, {}, {}
    for t in range(len(chains) + ATTN_PV_LAG):
        if t < len(chains):
            s_vals[t] = scores(t)
        if 0 <= t - ATTN_SOFTMAX_LAG < len(chains):
            p_vals[t - ATTN_SOFTMAX_LAG] = softmax(s_vals.pop(t - ATTN_SOFTMAX_LAG))
        if 0 <= t - ATTN_PV_LAG < len(chains):
            i = t - ATTN_PV_LAG
            o_vals[i] = weighted_v(i, *p_vals.pop(i))
            if i % 2 == 1:
                r0 = chains[i][0]
                o = o_vals.pop(i - 1) - lam * o_vals.pop(i)
                o_ref[r0:r0 + sub, :] = (_rms(o, g_ref[...]) * (1.0 - lambda_init)).astype(o_ref.dtype)


def _diff_attention(q, kv, lam_vecs, g_sub, *, batch, heads, hd, lambda_init):
    t, d = q.shape
    seq = t // batch
    head_block = pl.BlockSpec((seq, 2 * hd), lambda b, h: (b, h))
    return pl.pallas_call(
        functools.partial(_attn_kernel, hd=hd, lambda_init=lambda_init),
        grid=(batch, heads),
        in_specs=[
            head_block,
            head_block,
            pl.BlockSpec((seq, 2 * hd), lambda b, h: (b, heads + h)),
            pl.BlockSpec((8, hd), lambda b, h: (0, 0)),
            pl.BlockSpec((1, 2 * hd), lambda b, h: (0, 0)),
        ],
        out_specs=head_block,
        out_shape=jax.ShapeDtypeStruct((t, d), BF16),
        compiler_params=_params("parallel", "parallel"),
        name="diff_attn",
    )(q, kv, kv, lam_vecs, g_sub.reshape(1, 2 * hd))


def _cast_kernel(w_ref, o_ref):
    o_ref[...] = w_ref[...].astype(o_ref.dtype)


def _to_bf16(w, layer=None):
    if layer is None:
        w, layer = w[None], 0
    _, k, n = w.shape
    rows = max(8, min(k, CAST_BLOCK_ELEMS // n))
    assert k % rows == 0
    return pl.pallas_call(
        _cast_kernel,
        grid=(k // rows,),
        in_specs=[pl.BlockSpec((None, rows, n), lambda i: (layer, i, 0))],
        out_specs=pl.BlockSpec((rows, n), lambda i: (i, 0)),
        out_shape=jax.ShapeDtypeStruct((k, n), BF16),
        compiler_params=_params("parallel"),
        name="cast_bf16",
    )(w)


def _rope_tables(seq, hd):
    pos = jnp.arange(seq, dtype=F32)
    inv_freq = 1.0 / (ROPE_THETA ** (jnp.arange(0, hd, 2, dtype=F32) / hd))
    ang = pos[:, None] * inv_freq[None, :]
    emb = jnp.concatenate([ang, ang], axis=-1)
    sign = jnp.where(jnp.arange(hd) < hd // 2, -1.0, 1.0).astype(F32)
    return jnp.cos(emb), jnp.sin(emb) * sign


def kernel(x, mix_pre_g, mix_post_g, mlp_pre_g, mlp_post_g, ssm_w_in, ssm_a_re, ssm_a_im, ssm_log_dt, ssm_b_re, ssm_b_im, ssm_c_re, ssm_c_im, ssm_d, ssm_w_glu, kv_norm_g, w_kv, attn_w_q, lam_q1, lam_k1, lam_q2, lam_k2, attn_subln_g, attn_w_o, mlp_w_up, mlp_w_down):
    batch, seq, d = x.shape
    t = batch * seq
    depth = mix_pre_g.shape[0]
    n_a = ssm_w_in.shape[0]
    hd = lam_q1.shape[1]
    heads = d // (2 * hd)
    cos, sin = _rope_tables(seq, hd)

    h = x.reshape(t, d)
    kv = None
    for l in range(depth):
        if l < n_a:
            a = l
            u = _norm_matmul(h, mix_pre_g[l], _to_bf16(ssm_w_in, a), out_dtype=F32, bm=1024, name="ssm_in_proj")
            sw = _ssm_weights(ssm_a_re[a], ssm_a_im[a], ssm_log_dt[a], ssm_b_re[a], ssm_b_im[a],
                              ssm_c_re[a], ssm_c_im[a], SSM_TC)
            z = _ssm_scan(u, *sw, ssm_d[a], batch=batch)
            h = _matmul_norm_res(z, _to_bf16(ssm_w_glu, a), h, mix_post_g[l], bm=512, glu=True, name="ssm_glu_out")
        else:
            b = l - n_a
            lambda_init = 0.8 - 0.6 * math.exp(-0.3 * l)
            q = _norm_matmul(h, mix_pre_g[l], _to_bf16(attn_w_q, b), out_dtype=BF16, bm=1024, name="q_proj",
                             rope=(cos, sin), rope_cols=d, scale=hd ** -0.5)
            lam_vecs = jnp.zeros((8, hd), F32).at[0:4].set(
                jnp.stack([lam_q1[b], lam_k1[b], lam_q2[b], lam_k2[b]]).astype(F32))
            o = _diff_attention(q, kv, lam_vecs, attn_subln_g[b], batch=batch, heads=heads, hd=hd,
                                lambda_init=lambda_init)
            h = _matmul_norm_res(o, _to_bf16(attn_w_o, b), h, mix_post_g[l], bm=1024, name="attn_out_proj")
        h = _mlp(h, mlp_pre_g[l], _to_bf16(mlp_w_up, l), _to_bf16(mlp_w_down, l), mlp_post_g[l],
                 bm=1024, bf=1024)
        if l == n_a - 1:
            kv = _norm_matmul(h, kv_norm_g, _to_bf16(w_kv), out_dtype=BF16, bm=512, name="kv_proj",
                              rope=(cos, sin), rope_cols=d, scale=1.0)
    return h.reshape(batch, seq, d)
```

```python
import functools
import math

import jax
import jax.numpy as jnp
from jax import lax
from jax.experimental import pallas as pl
from jax.experimental.pallas import tpu as pltpu

EPS = 1e-6
ROPE_THETA = 10000.0
LANES = 128
SSM_TC = 8
SSM_SEQS = 4
VMEM_LIMIT_BYTES = 56 * 1024 * 1024
MLP_VMEM_LIMIT_BYTES = 58 * 1024 * 1024
MASK_VALUE = -1e30
MLP_DOWN_COLS = 512
ROW_SUB = 256
COL_SLAB = 512
CAST_BLOCK_ELEMS = 2 * 1024 * 1024
ROWS = {"ssm_in_proj": 1024, "ssm_glu_out": 512, "kv_proj": 1024, "q_proj": 1024, "attn_out_proj": 1024,
        "mlp": 1024}
MLP_HIDDEN_TILE = 1024
ATTN_SOFTMAX_LAG = 1
ATTN_PV_LAG = 2

F32 = jnp.float32
BF16 = jnp.bfloat16


def _params(*sem, vmem_limit_bytes=VMEM_LIMIT_BYTES):
    return pltpu.CompilerParams(dimension_semantics=sem, vmem_limit_bytes=vmem_limit_bytes)


def _rms(x, g):
    return x * lax.rsqrt(jnp.mean(x * x, axis=-1, keepdims=True) + EPS) * g


def _dot(a, b):
    return jnp.dot(a, b, preferred_element_type=F32)


def _rope(t, cos, sin_signed):
    return t * cos + pltpu.roll(t, LANES // 2, 1) * sin_signed


def _make_norm_matmul_kernel(rope_cols, scale):
    def kernel(x_ref, g_ref, w_ref, *rest):
        if rope_cols:
            cos_ref, sin_ref, o_ref = rest
        else:
            (o_ref,) = rest
        bm, n = o_ref.shape
        sub, slab = min(ROW_SUB, bm), min(COL_SLAB, n)
        g = g_ref[...]
        for r in range(bm // sub):
            rs = slice(r * sub, (r + 1) * sub)
            xn = _rms(x_ref[rs, :], g).astype(BF16)
            for c in range(n // slab):
                acc = _dot(xn, w_ref[:, c * slab:(c + 1) * slab])
                if c * slab < rope_cols:
                    cos, sin = cos_ref[rs, :], sin_ref[rs, :]
                    for cc in range(slab // LANES):
                        sl = slice(cc * LANES, (cc + 1) * LANES)
                        osl = slice(c * slab + cc * LANES, c * slab + (cc + 1) * LANES)
                        o_ref[rs, osl] = (_rope(acc[:, sl], cos, sin) * scale).astype(o_ref.dtype)
                else:
                    o_ref[rs, c * slab:(c + 1) * slab] = acc.astype(o_ref.dtype)

    return kernel


def _norm_matmul(x, g, w, *, out_dtype, bm, name, rope=None, rope_cols=0, scale=1.0):
    m, k = x.shape
    n = w.shape[1]
    bm = min(bm, m)
    args = [x, g.reshape(1, k), w]
    if rope is not None:
        cos, sin = rope
        seq = cos.shape[0]
        bm = min(bm, seq)
        assert rope_cols % min(COL_SLAB, n) == 0 and seq % bm == 0
        nseq = seq // bm
        args += [cos, sin]
    in_specs = [
        pl.BlockSpec((bm, k), lambda i: (i, 0)),
        pl.BlockSpec((1, k), lambda i: (0, 0)),
        pl.BlockSpec((k, n), lambda i: (0, 0), pipeline_mode=pl.Buffered(1)),
    ]
    if rope is not None:
        in_specs += [pl.BlockSpec((bm, LANES), lambda i: (i % nseq, 0))] * 2
    return pl.pallas_call(
        _make_norm_matmul_kernel(rope_cols, scale),
        grid=(m // bm,),
        in_specs=in_specs,
        out_specs=pl.BlockSpec((bm, n), lambda i: (i, 0)),
        out_shape=jax.ShapeDtypeStruct((m, n), out_dtype),
        compiler_params=_params("parallel"),
        name=name,
    )(*args)


def _ssm_prep_kernel(pw_ref, bb_ref, cc_ref, wi_ref, wn_ref, wo_ref, *, tc, p_ch, n_st):
    s_dim = pw_ref.shape[1]
    gpb = LANES // p_ch
    row_g = lax.shift_right_logical(lax.broadcasted_iota(jnp.int32, (LANES, s_dim), 0), int(math.log2(p_ch)))
    col_g = lax.shift_right_logical(lax.broadcasted_iota(jnp.int32, (LANES, s_dim), 1), int(math.log2(n_st)))
    same_group = row_g == col_g

    def expand(x):
        return jnp.where(same_group, jnp.concatenate([x] * gpb, axis=0), 0.0)

    b_r, b_i = expand(bb_ref[0]), expand(bb_ref[1])
    c_r, c_i = expand(cc_ref[0]), expand(cc_ref[1])
    def split(x):
        hi = x.astype(BF16)
        return hi, (x - hi.astype(F32)).astype(BF16)

    c0_hi, c0_lo = split(jnp.concatenate([c_r, -c_i], axis=1).T)
    kd = []
    for d in range(tc):
        pr, pi = pw_ref[d:d + 1, :], pw_ref[tc + 1 + d:tc + 2 + d, :]
        e_hi, e_lo = split(jnp.concatenate([pr * b_r - pi * b_i, pr * b_i + pi * b_r], axis=1))
        wn_ref[(tc - 1 - d) * LANES:(tc - d) * LANES, :] = e_hi
        kd.append((_dot(e_hi, c0_hi) + (_dot(e_lo, c0_hi) + _dot(e_hi, c0_lo))).astype(BF16))
    for t in range(tc):
        pr, pi = pw_ref[t + 1:t + 2, :], pw_ref[tc + 2 + t:tc + 3 + t, :]
        e = jnp.concatenate([pr * c_r - pi * c_i, -(pr * c_i + pi * c_r)], axis=1)
        wo_ref[:, t * LANES:(t + 1) * LANES] = e.T.astype(BF16)
    zero = jnp.zeros((LANES, LANES), BF16)
    for s in range(tc):
        for t in range(tc):
            wi_ref[s * LANES:(s + 1) * LANES, t * LANES:(t + 1) * LANES] = kd[t - s] if t >= s else zero


def _ssm_weights(a_re, a_im, log_dt, b_re, b_im, c_re, c_im, tc):
    g_n, n_st = a_re.shape
    p_ch = b_re.shape[-1]
    gpb = LANES // p_ch
    nj = g_n // gpb
    s_dim = gpb * n_st
    assert p_ch & (p_ch - 1) == 0 and n_st & (n_st - 1) == 0
    step = jnp.exp(log_dt.astype(F32))[:, None]
    lam_re = jnp.minimum(a_re.astype(F32), -1e-4)
    lam_im = a_im.astype(F32)
    mag = jnp.exp(step * lam_re)
    abar_re = mag * jnp.cos(step * lam_im)
    abar_im = mag * jnp.sin(step * lam_im)
    den = lam_re * lam_re + lam_im * lam_im
    nr = abar_re - 1.0
    ni = abar_im
    coef_re = (nr * lam_re + ni * lam_im) / den
    coef_im = (ni * lam_re - nr * lam_im) / den
    bre, bim = b_re.astype(F32), b_im.astype(F32)
    bbar_re = coef_re[..., None] * bre - coef_im[..., None] * bim
    bbar_im = coef_re[..., None] * bim + coef_im[..., None] * bre
    dd = jnp.arange(tc + 1, dtype=F32)[:, None, None]
    pmag = jnp.exp(dd * (step * lam_re))
    pw = jnp.concatenate([pmag * jnp.cos(dd * (step * lam_im)),
                          pmag * jnp.sin(dd * (step * lam_im))], axis=0)
    pw = pw.reshape(2 * (tc + 1), nj, s_dim).transpose(1, 0, 2)
    bb = jnp.stack([bbar_re, bbar_im]).reshape(2, nj, gpb, n_st, p_ch)
    bb = bb.transpose(1, 0, 4, 2, 3).reshape(nj, 2, p_ch, s_dim)
    cc = jnp.stack([c_re.astype(F32), c_im.astype(F32)]).reshape(2, nj, gpb, p_ch, n_st)
    cc = cc.transpose(1, 0, 3, 2, 4).reshape(nj, 2, p_ch, s_dim)
    k_dim = tc * LANES
    wshape = jax.ShapeDtypeStruct((nj, k_dim, k_dim), BF16)
    assert 2 * s_dim == k_dim
    w_intra, w_in, w_out = pl.pallas_call(
        functools.partial(_ssm_prep_kernel, tc=tc, p_ch=p_ch, n_st=n_st),
        grid=(nj,),
        in_specs=[
            pl.BlockSpec((None, 2 * (tc + 1), s_dim), lambda j: (j, 0, 0)),
            pl.BlockSpec((None, 2, p_ch, s_dim), lambda j: (j, 0, 0, 0)),
            pl.BlockSpec((None, 2, p_ch, s_dim), lambda j: (j, 0, 0, 0)),
        ],
        out_specs=[pl.BlockSpec((None, k_dim, k_dim), lambda j: (j, 0, 0))] * 3,
        out_shape=[wshape] * 3,
        compiler_params=_params("parallel"),
        name="ssm_prep",
    )(pw, bb, cc)
    a_step = jnp.stack([pw[:, tc], pw[:, 2 * tc + 1]], axis=1)
    return w_intra, w_in, w_out, a_step


def _gelu_tanh(x):
    return 0.5 * x * (1.0 + jnp.tanh(math.sqrt(2.0 / math.pi) * (x + 0.044715 * (x * x * x))))


def _ssm_kernel(u_ref, wi_ref, wn_ref, wo_ref, a_ref, d_ref, z_ref, st_ref, zs_ref, *, nb, nc):
    tc = SSM_TC
    rows = nb * nc
    nk = st_ref.shape[0] // 2

    def step_rows(s):
        return pl.ds(s, rows, stride=tc)

    lhs = jnp.concatenate([u_ref[step_rows(s), :].astype(BF16) for s in range(tc)], axis=1)
    contrib = _dot(lhs, wn_ref[...])
    for k in range(2 * nk):
        for b in range(nb):
            st_ref[k, pl.ds(b, nc, stride=nb), :] = contrib[b * nc:(b + 1) * nc, k * LANES:(k + 1) * LANES]

    a = a_ref[...]
    ar = [jnp.broadcast_to(a[0:1, k * LANES:(k + 1) * LANES], (nb, LANES)) for k in range(nk)]
    ai = [jnp.broadcast_to(a[1:2, k * LANES:(k + 1) * LANES], (nb, LANES)) for k in range(nk)]

    def step(c, carry):
        rws = pl.ds(pl.multiple_of(c * nb, nb), nb)
        out = []
        for k in range(nk):
            sr, si = carry[2 * k], carry[2 * k + 1]
            cr = st_ref[k, rws, :]
            ci = st_ref[nk + k, rws, :]
            st_ref[k, rws, :] = sr
            st_ref[nk + k, rws, :] = si
            out += [ar[k] * sr - ai[k] * si + cr, ar[k] * si + ai[k] * sr + ci]
        return tuple(out)

    y_intra = _dot(lhs, wi_ref[...])
    zero = jnp.zeros((nb, LANES), F32)
    lax.fori_loop(0, nc, step, (zero,) * (2 * nk), unroll=True)

    state = jnp.concatenate(
        [jnp.concatenate([st_ref[k, pl.ds(b, nc, stride=nb), :] for b in range(nb)], axis=0).astype(BF16)
         for k in range(2 * nk)], axis=1)
    y = y_intra + _dot(state, wo_ref[...])
    d = d_ref[...]
    for t in range(tc):
        yt = y[:, t * LANES:(t + 1) * LANES] + d * u_ref[step_rows(t), :]
        zs_ref[step_rows(t), :] = _gelu_tanh(yt)
    z_ref[...] = zs_ref[...].astype(z_ref.dtype)


def _ssm_scan(u, w_intra, w_in, w_out, a_step, d_skip, *, batch):
    t, d = u.shape
    seq = t // batch
    tc = SSM_TC
    nc = seq // tc
    nb = min(batch, SSM_SEQS)
    k_dim = tc * LANES
    return pl.pallas_call(
        functools.partial(_ssm_kernel, nb=nb, nc=nc),
        grid=(d // LANES, batch // nb),
        in_specs=[
            pl.BlockSpec((nb * seq, LANES), lambda j, b: (b, j)),
            pl.BlockSpec((None, k_dim, k_dim), lambda j, b: (j, 0, 0)),
            pl.BlockSpec((None, k_dim, k_dim), lambda j, b: (j, 0, 0)),
            pl.BlockSpec((None, k_dim, k_dim), lambda j, b: (j, 0, 0)),
            pl.BlockSpec((None, 2, k_dim // 2), lambda j, b: (j, 0, 0)),
            pl.BlockSpec((1, LANES), lambda j, b: (0, j)),
        ],
        out_specs=pl.BlockSpec((nb * seq, LANES), lambda j, b: (b, j)),
        out_shape=jax.ShapeDtypeStruct((t, d), BF16),
        scratch_shapes=[pltpu.VMEM((k_dim // LANES, nc * nb, LANES), F32),
                        pltpu.VMEM((nb * seq, LANES), F32)],
        compiler_params=_params("parallel", "parallel"),
        name="ssm_scan",
    )(u, w_intra, w_in, w_out, a_step, d_skip.reshape(1, d))


def _make_matmul_norm_res_kernel(glu):
    def kernel(a_ref, w_ref, x_ref, g_ref, o_ref):
        bm, d = o_ref.shape
        sub, slab = min(ROW_SUB, bm), min(COL_SLAB, d)
        g = g_ref[...]
        for r in range(bm // sub):
            rs = slice(r * sub, (r + 1) * sub)
            a = a_ref[rs, :]
            ssq = jnp.zeros((sub, 1), F32)
            for c in range(d // slab):
                cs = slice(c * slab, (c + 1) * slab)
                mix = _dot(a, w_ref[:, cs])
                if glu:
                    gate = _dot(a, w_ref[:, d + c * slab:d + (c + 1) * slab])
                    mix = mix * (1.0 / (1.0 + jnp.exp(-gate)))
                o_ref[rs, cs] = mix
                ssq = ssq + jnp.sum(mix * mix, axis=1, keepdims=True)
            o_ref[rs, :] = x_ref[rs, :] + o_ref[rs, :] * lax.rsqrt(ssq * (1.0 / d) + EPS) * g

    return kernel


def _matmul_norm_res(a, w, x, g, *, bm, name, glu=False):
    m, k = a.shape
    d = x.shape[1]
    bm = min(bm, m)
    return pl.pallas_call(
        _make_matmul_norm_res_kernel(glu),
        grid=(m // bm,),
        in_specs=[
            pl.BlockSpec((bm, k), lambda i: (i, 0)),
            pl.BlockSpec(w.shape, lambda i: (0, 0), pipeline_mode=pl.Buffered(1)),
            pl.BlockSpec((bm, d), lambda i: (i, 0)),
            pl.BlockSpec((1, d), lambda i: (0, 0)),
        ],
        out_specs=pl.BlockSpec((bm, d), lambda i: (i, 0)),
        out_shape=jax.ShapeDtypeStruct((m, d), F32),
        compiler_params=_params("parallel"),
        name=name,
    )(a, w, x, g.reshape(1, d))


def _mlp_kernel(x_ref, gpre_ref, wu_ref, wd_ref, gpost_ref, o_ref, xn_ref, *, nf):
    f = pl.program_id(1)
    bm, d = o_ref.shape
    bn = min(MLP_DOWN_COLS, d)

    def up_down(xn, rs, first):
        a = jnp.square(jnp.maximum(_dot(xn, wu_ref[...]), 0.0)).astype(BF16)
        for c in range(d // bn):
            sl = slice(c * bn, (c + 1) * bn)
            part = _dot(a, wd_ref[:, sl])
            if first:
                o_ref[rs, sl] = part
            else:
                o_ref[rs, sl] += part

    def edge_step(first, last):
        sub = min(ROW_SUB, bm)
        for r in range(bm // sub):
            rs = slice(r * sub, (r + 1) * sub)
            if first:
                xn = _rms(x_ref[rs, :], gpre_ref[...]).astype(BF16)
                xn_ref[rs, :] = xn
            else:
                xn = xn_ref[rs, :]
            up_down(xn, rs, first)
            if last:
                o_ref[rs, :] = x_ref[rs, :] + _rms(o_ref[rs, :], gpost_ref[...])

    @pl.when(f == 0)
    def _():
        edge_step(True, nf == 1)

    if nf > 2:
        @pl.when(jnp.logical_and(f > 0, f < nf - 1))
        def _():
            up_down(xn_ref[...], slice(None), False)

    if nf > 1:
        @pl.when(f == nf - 1)
        def _():
            edge_step(False, True)


def _mlp(x, g_pre, w_up, w_down, g_post, *, bm, bf):
    m, d = x.shape
    dff = w_up.shape[1]
    bm, bf = min(bm, m), min(bf, dff)
    return pl.pallas_call(
        functools.partial(_mlp_kernel, nf=dff // bf),
        grid=(m // bm, dff // bf),
        in_specs=[
            pl.BlockSpec((bm, d), lambda i, f: (i, 0), pipeline_mode=pl.Buffered(1)),
            pl.BlockSpec((1, d), lambda i, f: (0, 0)),
            pl.BlockSpec((d, bf), lambda i, f: (0, f)),
            pl.BlockSpec((bf, d), lambda i, f: (f, 0)),
            pl.BlockSpec((1, d), lambda i, f: (0, 0)),
        ],
        out_specs=pl.BlockSpec((bm, d), lambda i, f: (i, 0)),
        out_shape=jax.ShapeDtypeStruct((m, d), F32),
        scratch_shapes=[pltpu.VMEM((bm, d), BF16)],
        compiler_params=_params("parallel", "arbitrary", vmem_limit_bytes=MLP_VMEM_LIMIT_BYTES),
        name="mlp",
    )(x, g_pre.reshape(1, d), w_up, w_down, g_post.reshape(1, d))


def _attn_kernel(q_ref, k_ref, v_ref, lam_ref, g_ref, o_ref, *, hd, lambda_init):
    seq = q_ref.shape[0]
    sub = min(ROW_SUB, seq)
    nt = (((1,), (1,)), ((), ()))
    lv = lam_ref[...]
    lam = (jnp.exp(jnp.sum(lv[0:1] * lv[1:2], axis=1, keepdims=True))
           - jnp.exp(jnp.sum(lv[2:3] * lv[3:4], axis=1, keepdims=True)) + lambda_init)
    row = lax.broadcasted_iota(jnp.int32, (sub, sub), 0)
    col = lax.broadcasted_iota(jnp.int32, (sub, sub), 1)
    causal = col <= row

    chains = [(r0, c0) for r0 in reversed(range(0, seq, sub)) for c0 in (0, hd)]

    def scores(i):
        r0, c0 = chains[i]
        kv_len = r0 + sub
        s = lax.dot_general(q_ref[r0:r0 + sub, c0:c0 + hd], k_ref[0:kv_len, c0:c0 + hd], nt,
                            preferred_element_type=F32)
        s_diag = jnp.where(causal, s[:, kv_len - sub:], MASK_VALUE)
        return s_diag if kv_len == sub else jnp.concatenate([s[:, :kv_len - sub], s_diag], axis=1)

    def softmax(s):
        p = jnp.exp(s - jnp.max(s, axis=1, keepdims=True))
        return p.astype(BF16), jnp.sum(p, axis=1, keepdims=True)

    def weighted_v(i, p, l):
        return _dot(p, v_ref[0:chains[i][0] + sub, :]) * (1.0 / l)

    s_vals, p_vals, o_vals = {}, {}, {}
    for t in range(len(chains) + ATTN_PV_LAG):
        if t < len(chains):
            s_vals[t] = scores(t)
        if 0 <= t - ATTN_SOFTMAX_LAG < len(chains):
            p_vals[t - ATTN_SOFTMAX_LAG] = softmax(s_vals.pop(t - ATTN_SOFTMAX_LAG))
        if 0 <= t - ATTN_PV_LAG < len(chains):
            i = t - ATTN_PV_LAG
            o_vals[i] = weighted_v(i, *p_vals.pop(i))
            if i % 2 == 1:
                r0 = chains[i][0]
                o = o_vals.pop(i - 1) - lam * o_vals.pop(i)
                o_ref[r0:r0 + sub, :] = (_rms(o, g_ref[...]) * (1.0 - lambda_init)).astype(o_ref.dtype)


def _diff_attention(q, kv, lam_vecs, g_sub, *, batch, heads, hd, lambda_init):
    t, d = q.shape
    seq = t // batch
    head_block = pl.BlockSpec((seq, 2 * hd), lambda b, h: (b, h))
    return pl.pallas_call(
        functools.partial(_attn_kernel, hd=hd, lambda_init=lambda_init),
        grid=(batch, heads),
        in_specs=[
            head_block,
            head_block,
            pl.BlockSpec((seq, 2 * hd), lambda b, h: (b, heads + h)),
            pl.BlockSpec((8, hd), lambda b, h: (0, 0)),
            pl.BlockSpec((1, 2 * hd), lambda b, h: (0, 0)),
        ],
        out_specs=head_block,
        out_shape=jax.ShapeDtypeStruct((t, d), BF16),
        compiler_params=_params("parallel", "parallel"),
        name="diff_attn",
    )(q, kv, kv, lam_vecs, g_sub.reshape(1, 2 * hd))


def _cast_kernel(w_ref, o_ref):
    o_ref[...] = w_ref[...].astype(o_ref.dtype)


def _to_bf16(w, layer=None):
    if layer is None:
        w, layer = w[None], 0
    _, k, n = w.shape
    rows = max(8, min(k, CAST_BLOCK_ELEMS // n))
    assert k % rows == 0
    return pl.pallas_call(
        _cast_kernel,
        grid=(k // rows,),
        in_specs=[pl.BlockSpec((None, rows, n), lambda i: (layer, i, 0))],
        out_specs=pl.BlockSpec((rows, n), lambda i: (i, 0)),
        out_shape=jax.ShapeDtypeStruct((k, n), BF16),
        compiler_params=_params("parallel"),
        name="cast_bf16",
    )(w)


def _rope_tables(seq, hd):
    pos = jnp.arange(seq, dtype=F32)
    inv_freq = 1.0 / (ROPE_THETA ** (jnp.arange(0, hd, 2, dtype=F32) / hd))
    ang = pos[:, None] * inv_freq[None, :]
    emb = jnp.concatenate([ang, ang], axis=-1)
    sign = jnp.where(jnp.arange(hd) < hd // 2, -1.0, 1.0).astype(F32)
    return jnp.cos(emb), jnp.sin(emb) * sign


def kernel(x, mix_pre_g, mix_post_g, mlp_pre_g, mlp_post_g, ssm_w_in, ssm_a_re, ssm_a_im, ssm_log_dt, ssm_b_re, ssm_b_im, ssm_c_re, ssm_c_im, ssm_d, ssm_w_glu, kv_norm_g, w_kv, attn_w_q, lam_q1, lam_k1, lam_q2, lam_k2, attn_subln_g, attn_w_o, mlp_w_up, mlp_w_down):
    batch, seq, d = x.shape
    t = batch * seq
    depth = mix_pre_g.shape[0]
    n_a = ssm_w_in.shape[0]
    hd = lam_q1.shape[1]
    heads = d // (2 * hd)
    cos, sin = _rope_tables(seq, hd)

    h = x.reshape(t, d)
    kv = None
    for l in range(depth):
        if l < n_a:
            a = l
            u = _norm_matmul(h, mix_pre_g[l], _to_bf16(ssm_w_in, a), out_dtype=F32, bm=ROWS["ssm_in_proj"], name="ssm_in_proj")
            sw = _ssm_weights(ssm_a_re[a], ssm_a_im[a], ssm_log_dt[a], ssm_b_re[a], ssm_b_im[a],
                              ssm_c_re[a], ssm_c_im[a], SSM_TC)
            z = _ssm_scan(u, *sw, ssm_d[a], batch=batch)
            h = _matmul_norm_res(z, _to_bf16(ssm_w_glu, a), h, mix_post_g[l], bm=ROWS["ssm_glu_out"], glu=True, name="ssm_glu_out")
        else:
            b = l - n_a
            lambda_init = 0.8 - 0.6 * math.exp(-0.3 * l)
            q = _norm_matmul(h, mix_pre_g[l], _to_bf16(attn_w_q, b), out_dtype=BF16, bm=ROWS["q_proj"], name="q_proj",
                             rope=(cos, sin), rope_cols=d, scale=hd ** -0.5)
            lam_vecs = jnp.zeros((8, hd), F32).at[0:4].set(
                jnp.stack([lam_q1[b], lam_k1[b], lam_q2[b], lam_k2[b]]).astype(F32))
            o = _diff_attention(q, kv, lam_vecs, attn_subln_g[b], batch=batch, heads=heads, hd=hd,
                                lambda_init=lambda_init)
            h = _matmul_norm_res(o, _to_bf16(attn_w_o, b), h, mix_post_g[l], bm=ROWS["attn_out_proj"], name="attn_out_proj")
        h = _mlp(h, mlp_pre_g[l], _to_bf16(mlp_w_up, l), _to_bf16(mlp_w_down, l), mlp_post_g[l],
                 bm=ROWS["mlp"], bf=MLP_HIDDEN_TILE)
        if l == n_a - 1:
            kv = _norm_matmul(h, kv_norm_g, _to_bf16(w_kv), out_dtype=BF16, bm=ROWS["kv_proj"], name="kv_proj",
                              rope=(cos, sin), rope_cols=d, scale=1.0)
    return h.reshape(batch, seq, d)
```

```python
import functools
import math

import jax
import jax.numpy as jnp
from jax import lax
from jax.experimental import pallas as pl
from jax.experimental.pallas import tpu as pltpu

EPS = 1e-6
ROPE_THETA = 10000.0
LANES = 128
SSM_TC = 8
SSM_SEQS = 4
VMEM_LIMIT_BYTES = 56 * 1024 * 1024
MLP_VMEM_LIMIT_BYTES = 58 * 1024 * 1024
MASK_VALUE = -1e30
MLP_DOWN_COLS = 512
ROW_SUB = 256
COL_SLAB = 512
CAST_BLOCK_ELEMS = 2 * 1024 * 1024
ROWS = {"ssm_in_proj": 1024, "ssm_glu_out": 512, "kv_proj": 1024, "q_proj": 1024, "attn_out_proj": 1024,
        "mlp": 1024}
MLP_HIDDEN_TILE = 1024
ATTN_SOFTMAX_LAG = 1
ATTN_PV_LAG = 2

F32 = jnp.float32
BF16 = jnp.bfloat16


def _params(*sem, vmem_limit_bytes=VMEM_LIMIT_BYTES):
    return pltpu.CompilerParams(dimension_semantics=sem, vmem_limit_bytes=vmem_limit_bytes)


def _rms(x, g):
    return x * lax.rsqrt(jnp.mean(x * x, axis=-1, keepdims=True) + EPS) * g


def _dot(a, b):
    return jnp.dot(a, b, preferred_element_type=F32)


def _rope(t, cos, sin_signed):
    return t * cos + pltpu.roll(t, LANES // 2, 1) * sin_signed


def _make_norm_matmul_kernel(rope_cols, scale):
    def kernel(x_ref, g_ref, w_ref, *rest):
        if rope_cols:
            cos_ref, sin_ref, o_ref = rest
        else:
            (o_ref,) = rest
        bm, n = o_ref.shape
        sub, slab = min(ROW_SUB, bm), min(COL_SLAB, n)
        g = g_ref[...]
        for r in range(bm // sub):
            rs = slice(r * sub, (r + 1) * sub)
            xn = _rms(x_ref[rs, :], g).astype(BF16)
            for c in range(n // slab):
                acc = _dot(xn, w_ref[:, c * slab:(c + 1) * slab])
                if c * slab < rope_cols:
                    cos, sin = cos_ref[rs, :], sin_ref[rs, :]
                    for cc in range(slab // LANES):
                        sl = slice(cc * LANES, (cc + 1) * LANES)
                        osl = slice(c * slab + cc * LANES, c * slab + (cc + 1) * LANES)
                        o_ref[rs, osl] = (_rope(acc[:, sl], cos, sin) * scale).astype(o_ref.dtype)
                else:
                    o_ref[rs, c * slab:(c + 1) * slab] = acc.astype(o_ref.dtype)

    return kernel


def _norm_matmul(x, g, w, *, out_dtype, bm, name, rope=None, rope_cols=0, scale=1.0):
    m, k = x.shape
    n = w.shape[1]
    bm = min(bm, m)
    args = [x, g.reshape(1, k), w]
    if rope is not None:
        cos, sin = rope
        seq = cos.shape[0]
        bm = min(bm, seq)
        assert rope_cols % min(COL_SLAB, n) == 0 and seq % bm == 0
        nseq = seq // bm
        args += [cos, sin]
    in_specs = [
        pl.BlockSpec((bm, k), lambda i: (i, 0)),
        pl.BlockSpec((1, k), lambda i: (0, 0)),
        pl.BlockSpec((k, n), lambda i: (0, 0), pipeline_mode=pl.Buffered(1)),
    ]
    if rope is not None:
        in_specs += [pl.BlockSpec((bm, LANES), lambda i: (i % nseq, 0))] * 2
    return pl.pallas_call(
        _make_norm_matmul_kernel(rope_cols, scale),
        grid=(m // bm,),
        in_specs=in_specs,
        out_specs=pl.BlockSpec((bm, n), lambda i: (i, 0)),
        out_shape=jax.ShapeDtypeStruct((m, n), out_dtype),
        compiler_params=_params("parallel"),
        name=name,
    )(*args)


def _ssm_prep_kernel(pw_ref, bb_ref, cc_ref, wi_ref, wn_ref, wo_ref, *, tc, p_ch, n_st):
    s_dim = pw_ref.shape[1]
    gpb = LANES // p_ch
    row_g = lax.shift_right_logical(lax.broadcasted_iota(jnp.int32, (LANES, s_dim), 0), int(math.log2(p_ch)))
    col_g = lax.shift_right_logical(lax.broadcasted_iota(jnp.int32, (LANES, s_dim), 1), int(math.log2(n_st)))
    same_group = row_g == col_g

    def expand(x):
        return jnp.where(same_group, jnp.concatenate([x] * gpb, axis=0), 0.0)

    b_r, b_i = expand(bb_ref[0]), expand(bb_ref[1])
    c_r, c_i = expand(cc_ref[0]), expand(cc_ref[1])
    def split(x):
        hi = x.astype(BF16)
        return hi, (x - hi.astype(F32)).astype(BF16)

    c0_hi, c0_lo = split(jnp.concatenate([c_r, -c_i], axis=1).T)
    kd = []
    for d in range(tc):
        pr, pi = pw_ref[d:d + 1, :], pw_ref[tc + 1 + d:tc + 2 + d, :]
        e_hi, e_lo = split(jnp.concatenate([pr * b_r - pi * b_i, pr * b_i + pi * b_r], axis=1))
        wn_ref[(tc - 1 - d) * LANES:(tc - d) * LANES, :] = e_hi
        kd.append((_dot(e_hi, c0_hi) + (_dot(e_lo, c0_hi) + _dot(e_hi, c0_lo))).astype(BF16))
    for t in range(tc):
        pr, pi = pw_ref[t + 1:t + 2, :], pw_ref[tc + 2 + t:tc + 3 + t, :]
        e = jnp.concatenate([pr * c_r - pi * c_i, -(pr * c_i + pi * c_r)], axis=1)
        wo_ref[:, t * LANES:(t + 1) * LANES] = e.T.astype(BF16)
    zero = jnp.zeros((LANES, LANES), BF16)
    for s in range(tc):
        for t in range(tc):
            wi_ref[s * LANES:(s + 1) * LANES, t * LANES:(t + 1) * LANES] = kd[t - s] if t >= s else zero


def _ssm_weights(a_re, a_im, log_dt, b_re, b_im, c_re, c_im, tc):
    g_n, n_st = a_re.shape
    p_ch = b_re.shape[-1]
    gpb = LANES // p_ch
    nj = g_n // gpb
    s_dim = gpb * n_st
    assert p_ch & (p_ch - 1) == 0 and n_st & (n_st - 1) == 0
    step = jnp.exp(log_dt.astype(F32))[:, None]
    lam_re = jnp.minimum(a_re.astype(F32), -1e-4)
    lam_im = a_im.astype(F32)
    mag = jnp.exp(step * lam_re)
    abar_re = mag * jnp.cos(step * lam_im)
    abar_im = mag * jnp.sin(step * lam_im)
    den = lam_re * lam_re + lam_im * lam_im
    nr = abar_re - 1.0
    ni = abar_im
    coef_re = (nr * lam_re + ni * lam_im) / den
    coef_im = (ni * lam_re - nr * lam_im) / den
    bre, bim = b_re.astype(F32), b_im.astype(F32)
    bbar_re = coef_re[..., None] * bre - coef_im[..., None] * bim
    bbar_im = coef_re[..., None] * bim + coef_im[..., None] * bre
    dd = jnp.arange(tc + 1, dtype=F32)[:, None, None]
    pmag = jnp.exp(dd * (step * lam_re))
    pw = jnp.concatenate([pmag * jnp.cos(dd * (step * lam_im)),
                          pmag * jnp.sin(dd * (step * lam_im))], axis=0)
    pw = pw.reshape(2 * (tc + 1), nj, s_dim).transpose(1, 0, 2)
    bb = jnp.stack([bbar_re, bbar_im]).reshape(2, nj, gpb, n_st, p_ch)
    bb = bb.transpose(1, 0, 4, 2, 3).reshape(nj, 2, p_ch, s_dim)
    cc = jnp.stack([c_re.astype(F32), c_im.astype(F32)]).reshape(2, nj, gpb, p_ch, n_st)
    cc = cc.transpose(1, 0, 3, 2, 4).reshape(nj, 2, p_ch, s_dim)
    k_dim = tc * LANES
    wshape = jax.ShapeDtypeStruct((nj, k_dim, k_dim), BF16)
    assert 2 * s_dim == k_dim
    w_intra, w_in, w_out = pl.pallas_call(
        functools.partial(_ssm_prep_kernel, tc=tc, p_ch=p_ch, n_st=n_st),
        grid=(nj,),
        in_specs=[
            pl.BlockSpec((None, 2 * (tc + 1), s_dim), lambda j: (j, 0, 0)),
            pl.BlockSpec((None, 2, p_ch, s_dim), lambda j: (j, 0, 0, 0)),
            pl.BlockSpec((None, 2, p_ch, s_dim), lambda j: (j, 0, 0, 0)),
        ],
        out_specs=[pl.BlockSpec((None, k_dim, k_dim), lambda j: (j, 0, 0))] * 3,
        out_shape=[wshape] * 3,
        compiler_params=_params("parallel"),
        name="ssm_prep",
    )(pw, bb, cc)
    a_step = jnp.stack([pw[:, tc], pw[:, 2 * tc + 1]], axis=1)
    return w_intra, w_in, w_out, a_step


def _gelu_tanh(x):
    return 0.5 * x * (1.0 + jnp.tanh(math.sqrt(2.0 / math.pi) * (x + 0.044715 * (x * x * x))))


def _ssm_kernel(u_ref, wi_ref, wn_ref, wo_ref, a_ref, d_ref, z_ref, st_ref, zs_ref, *, nb, nc):
    tc = SSM_TC
    rows = nb * nc
    nk = st_ref.shape[0] // 2

    def step_rows(s):
        return pl.ds(s, rows, stride=tc)

    lhs = jnp.concatenate([u_ref[step_rows(s), :].astype(BF16) for s in range(tc)], axis=1)
    contrib = _dot(lhs, wn_ref[...])
    for k in range(2 * nk):
        for b in range(nb):
            st_ref[k, pl.ds(b, nc, stride=nb), :] = contrib[b * nc:(b + 1) * nc, k * LANES:(k + 1) * LANES]

    a = a_ref[...]
    ar = [jnp.broadcast_to(a[0:1, k * LANES:(k + 1) * LANES], (nb, LANES)) for k in range(nk)]
    ai = [jnp.broadcast_to(a[1:2, k * LANES:(k + 1) * LANES], (nb, LANES)) for k in range(nk)]

    def step(c, carry):
        rws = pl.ds(pl.multiple_of(c * nb, nb), nb)
        out = []
        for k in range(nk):
            sr, si = carry[2 * k], carry[2 * k + 1]
            cr = st_ref[k, rws, :]
            ci = st_ref[nk + k, rws, :]
            st_ref[k, rws, :] = sr
            st_ref[nk + k, rws, :] = si
            out += [ar[k] * sr - ai[k] * si + cr, ar[k] * si + ai[k] * sr + ci]
        return tuple(out)

    y_intra = _dot(lhs, wi_ref[...])
    zero = jnp.zeros((nb, LANES), F32)
    lax.fori_loop(0, nc, step, (zero,) * (2 * nk), unroll=True)

    state = jnp.concatenate(
        [jnp.concatenate([st_ref[k, pl.ds(b, nc, stride=nb), :] for b in range(nb)], axis=0).astype(BF16)
         for k in range(2 * nk)], axis=1)
    y = y_intra + _dot(state, wo_ref[...])
    d = d_ref[...]
    for t in range(tc):
        yt = y[:, t * LANES:(t + 1) * LANES] + d * u_ref[step_rows(t), :]
        zs_ref[step_rows(t), :] = _gelu_tanh(yt)
    z_ref[...] = zs_ref[...].astype(z_ref.dtype)


def _ssm_scan(u, w_intra, w_in, w_out, a_step, d_skip, *, batch):
    t, d = u.shape
    seq = t // batch
    tc = SSM_TC
    nc = seq // tc
    nb = min(batch, SSM_SEQS)
    k_dim = tc * LANES
    return pl.pallas_call(
        functools.partial(_ssm_kernel, nb=nb, nc=nc),
        grid=(d // LANES, batch // nb),
        in_specs=[
            pl.BlockSpec((nb * seq, LANES), lambda j, b: (b, j)),
            pl.BlockSpec((None, k_dim, k_dim), lambda j, b: (j, 0, 0)),
            pl.BlockSpec((None, k_dim, k_dim), lambda j, b: (j, 0, 0)),
            pl.BlockSpec((None, k_dim, k_dim), lambda j, b: (j, 0, 0)),
            pl.BlockSpec((None, 2, k_dim // 2), lambda j, b: (j, 0, 0)),
            pl.BlockSpec((1, LANES), lambda j, b: (0, j)),
        ],
        out_specs=pl.BlockSpec((nb * seq, LANES), lambda j, b: (b, j)),
        out_shape=jax.ShapeDtypeStruct((t, d), BF16),
        scratch_shapes=[pltpu.VMEM((k_dim // LANES, nc * nb, LANES), F32),
                        pltpu.VMEM((nb * seq, LANES), F32)],
        compiler_params=_params("parallel", "parallel"),
        name="ssm_scan",
    )(u, w_intra, w_in, w_out, a_step, d_skip.reshape(1, d))


def _make_matmul_norm_res_kernel(glu):
    def kernel(a_ref, w_ref, x_ref, g_ref, o_ref):
        bm, d = o_ref.shape
        sub, slab = min(ROW_SUB, bm), min(COL_SLAB, d)
        g = g_ref[...]
        for r in range(bm // sub):
            rs = slice(r * sub, (r + 1) * sub)
            a = a_ref[rs, :]
            ssq = jnp.zeros((sub, 1), F32)
            for c in range(d // slab):
                cs = slice(c * slab, (c + 1) * slab)
                mix = _dot(a, w_ref[:, cs])
                if glu:
                    gate = _dot(a, w_ref[:, d + c * slab:d + (c + 1) * slab])
                    mix = mix * (1.0 / (1.0 + jnp.exp(-gate)))
                o_ref[rs, cs] = mix
                ssq = ssq + jnp.sum(mix * mix, axis=1, keepdims=True)
            o_ref[rs, :] = x_ref[rs, :] + o_ref[rs, :] * lax.rsqrt(ssq * (1.0 / d) + EPS) * g

    return kernel


def _matmul_norm_res(a, w, x, g, *, bm, name, glu=False):
    m, k = a.shape
    d = x.shape[1]
    bm = min(bm, m)
    return pl.pallas_call(
        _make_matmul_norm_res_kernel(glu),
        grid=(m // bm,),
        in_specs=[
            pl.BlockSpec((bm, k), lambda i: (i, 0)),
            pl.BlockSpec(w.shape, lambda i: (0, 0), pipeline_mode=pl.Buffered(1)),
            pl.BlockSpec((bm, d), lambda i: (i, 0)),
            pl.BlockSpec((1, d), lambda i: (0, 0)),
        ],
        out_specs=pl.BlockSpec((bm, d), lambda i: (i, 0)),
        out_shape=jax.ShapeDtypeStruct((m, d), F32),
        compiler_params=_params("parallel"),
        name=name,
    )(a, w, x, g.reshape(1, d))


def _mlp_kernel(x_ref, gpre_ref, wu_ref, wd_ref, gpost_ref, o_ref, xn_ref, *, nf):
    f = pl.program_id(1)
    bm, d = o_ref.shape
    bn = min(MLP_DOWN_COLS, d)

    def up_down(xn, rs, first):
        a = jnp.square(jnp.maximum(_dot(xn, wu_ref[...]), 0.0)).astype(BF16)
        for c in range(d // bn):
            sl = slice(c * bn, (c + 1) * bn)
            part = _dot(a, wd_ref[:, sl])
            if first:
                o_ref[rs, sl] = part
            else:
                o_ref[rs, sl] += part

    def edge_step(first, last):
        sub = min(ROW_SUB, bm)
        for r in range(bm // sub):
            rs = slice(r * sub, (r + 1) * sub)
            if first:
                xn = _rms(x_ref[rs, :], gpre_ref[...]).astype(BF16)
                xn_ref[rs, :] = xn
            else:
                xn = xn_ref[rs, :]
            up_down(xn, rs, first)
            if last:
                o_ref[rs, :] = x_ref[rs, :] + _rms(o_ref[rs, :], gpost_ref[...])

    @pl.when(f == 0)
    def _():
        edge_step(True, nf == 1)

    if nf > 2:
        @pl.when(jnp.logical_and(f > 0, f < nf - 1))
        def _():
            up_down(xn_ref[...], slice(None), False)

    if nf > 1:
        @pl.when(f == nf - 1)
        def _():
            edge_step(False, True)


def _mlp(x, g_pre, w_up, w_down, g_post, *, bm):
    m, d = x.shape
    nf, _, bf = w_up.shape
    dff = nf * bf
    bm = min(bm, m)
    return pl.pallas_call(
        functools.partial(_mlp_kernel, nf=dff // bf),
        grid=(m // bm, dff // bf),
        in_specs=[
            pl.BlockSpec((bm, d), lambda i, f: (i, 0), pipeline_mode=pl.Buffered(1)),
            pl.BlockSpec((1, d), lambda i, f: (0, 0)),
            pl.BlockSpec((None, d, bf), lambda i, f: (f, 0, 0)),
            pl.BlockSpec((bf, d), lambda i, f: (f, 0)),
            pl.BlockSpec((1, d), lambda i, f: (0, 0)),
        ],
        out_specs=pl.BlockSpec((bm, d), lambda i, f: (i, 0)),
        out_shape=jax.ShapeDtypeStruct((m, d), F32),
        scratch_shapes=[pltpu.VMEM((bm, d), BF16)],
        compiler_params=_params("parallel", "arbitrary", vmem_limit_bytes=MLP_VMEM_LIMIT_BYTES),
        name="mlp",
    )(x, g_pre.reshape(1, d), w_up, w_down, g_post.reshape(1, d))


def _attn_kernel(q_ref, k_ref, v_ref, lam_ref, g_ref, o_ref, *, hd, lambda_init):
    seq = q_ref.shape[0]
    sub = min(ROW_SUB, seq)
    nt = (((1,), (1,)), ((), ()))
    lv = lam_ref[...]
    lam = (jnp.exp(jnp.sum(lv[0:1] * lv[1:2], axis=1, keepdims=True))
           - jnp.exp(jnp.sum(lv[2:3] * lv[3:4], axis=1, keepdims=True)) + lambda_init)
    row = lax.broadcasted_iota(jnp.int32, (sub, sub), 0)
    col = lax.broadcasted_iota(jnp.int32, (sub, sub), 1)
    causal = col <= row

    chains = [(r0, c0) for r0 in reversed(range(0, seq, sub)) for c0 in (0, hd)]

    def scores(i):
        r0, c0 = chains[i]
        kv_len = r0 + sub
        s = lax.dot_general(q_ref[r0:r0 + sub, c0:c0 + hd], k_ref[0:kv_len, c0:c0 + hd], nt,
                            preferred_element_type=F32)
        s_diag = jnp.where(causal, s[:, kv_len - sub:], MASK_VALUE)
        return s_diag if kv_len == sub else jnp.concatenate([s[:, :kv_len - sub], s_diag], axis=1)

    def softmax(s):
        p = jnp.exp(s - jnp.max(s, axis=1, keepdims=True))
        return p.astype(BF16), jnp.sum(p, axis=1, keepdims=True)

    def weighted_v(i, p, l):
        return _dot(p, v_ref[0:chains[i][0] + sub, :]) * (1.0 / l)

    s_vals, p_vals, o_vals = {}, {}, {}
    for t in range(len(chains) + ATTN_PV_LAG):
        if t < len(chains):
            s_vals[t] = scores(t)
        if 0 <= t - ATTN_SOFTMAX_LAG < len(chains):
            p_vals[t - ATTN_SOFTMAX_LAG] = softmax(s_vals.pop(t - ATTN_SOFTMAX_LAG))
        if 0 <= t - ATTN_PV_LAG < len(chains):
            i = t - ATTN_PV_LAG
            o_vals[i] = weighted_v(i, *p_vals.pop(i))
            if i % 2 == 1:
                r0 = chains[i][0]
                o = o_vals.pop(i - 1) - lam * o_vals.pop(i)
                o_ref[r0:r0 + sub, :] = (_rms(o, g_ref[...]) * (1.0 - lambda_init)).astype(o_ref.dtype)


def _diff_attention(q, kv, lam_vecs, g_sub, *, batch, heads, hd, lambda_init):
    t, d = q.shape
    seq = t // batch
    head_block = pl.BlockSpec((seq, 2 * hd), lambda b, h: (b, h))
    return pl.pallas_call(
        functools.partial(_attn_kernel, hd=hd, lambda_init=lambda_init),
        grid=(batch, heads),
        in_specs=[
            head_block,
            head_block,
            pl.BlockSpec((seq, 2 * hd), lambda b, h: (b, heads + h)),
            pl.BlockSpec((8, hd), lambda b, h: (0, 0)),
            pl.BlockSpec((1, 2 * hd), lambda b, h: (0, 0)),
        ],
        out_specs=head_block,
        out_shape=jax.ShapeDtypeStruct((t, d), BF16),
        compiler_params=_params("parallel", "parallel"),
        name="diff_attn",
    )(q, kv, kv, lam_vecs, g_sub.reshape(1, 2 * hd))


def _cast_kernel(w_ref, o_ref):
    if len(o_ref.shape) == 2:
        o_ref[...] = w_ref[...].astype(o_ref.dtype)
    else:
        tile = o_ref.shape[2]
        for c in range(o_ref.shape[0]):
            o_ref[c] = w_ref[:, c * tile:(c + 1) * tile].astype(o_ref.dtype)


def _to_bf16(w, layer=None, col_tile=None):
    if layer is None:
        w, layer = w[None], 0
    _, k, n = w.shape
    rows = max(8, min(k, CAST_BLOCK_ELEMS // n))
    assert k % rows == 0
    if col_tile is None:
        out_spec = pl.BlockSpec((rows, n), lambda i: (i, 0))
        out_shape = jax.ShapeDtypeStruct((k, n), BF16)
    else:
        out_spec = pl.BlockSpec((n // col_tile, rows, col_tile), lambda i: (0, i, 0))
        out_shape = jax.ShapeDtypeStruct((n // col_tile, k, col_tile), BF16)
    return pl.pallas_call(
        _cast_kernel,
        grid=(k // rows,),
        in_specs=[pl.BlockSpec((None, rows, n), lambda i: (layer, i, 0))],
        out_specs=out_spec,
        out_shape=out_shape,
        compiler_params=_params("parallel"),
        name="cast_bf16",
    )(w)


def _rope_tables(seq, hd):
    pos = jnp.arange(seq, dtype=F32)
    inv_freq = 1.0 / (ROPE_THETA ** (jnp.arange(0, hd, 2, dtype=F32) / hd))
    ang = pos[:, None] * inv_freq[None, :]
    emb = jnp.concatenate([ang, ang], axis=-1)
    sign = jnp.where(jnp.arange(hd) < hd // 2, -1.0, 1.0).astype(F32)
    return jnp.cos(emb), jnp.sin(emb) * sign


def kernel(x, mix_pre_g, mix_post_g, mlp_pre_g, mlp_post_g, ssm_w_in, ssm_a_re, ssm_a_im, ssm_log_dt, ssm_b_re, ssm_b_im, ssm_c_re, ssm_c_im, ssm_d, ssm_w_glu, kv_norm_g, w_kv, attn_w_q, lam_q1, lam_k1, lam_q2, lam_k2, attn_subln_g, attn_w_o, mlp_w_up, mlp_w_down):
    batch, seq, d = x.shape
    t = batch * seq
    depth = mix_pre_g.shape[0]
    n_a = ssm_w_in.shape[0]
    hd = lam_q1.shape[1]
    heads = d // (2 * hd)
    cos, sin = _rope_tables(seq, hd)

    h = x.reshape(t, d)
    kv = None
    for l in range(depth):
        if l < n_a:
            a = l
            u = _norm_matmul(h, mix_pre_g[l], _to_bf16(ssm_w_in, a), out_dtype=F32, bm=ROWS["ssm_in_proj"], name="ssm_in_proj")
            sw = _ssm_weights(ssm_a_re[a], ssm_a_im[a], ssm_log_dt[a], ssm_b_re[a], ssm_b_im[a],
                              ssm_c_re[a], ssm_c_im[a], SSM_TC)
            z = _ssm_scan(u, *sw, ssm_d[a], batch=batch)
            h = _matmul_norm_res(z, _to_bf16(ssm_w_glu, a), h, mix_post_g[l], bm=ROWS["ssm_glu_out"], glu=True, name="ssm_glu_out")
        else:
            b = l - n_a
            lambda_init = 0.8 - 0.6 * math.exp(-0.3 * l)
            q = _norm_matmul(h, mix_pre_g[l], _to_bf16(attn_w_q, b), out_dtype=BF16, bm=ROWS["q_proj"], name="q_proj",
                             rope=(cos, sin), rope_cols=d, scale=hd ** -0.5)
            lam_vecs = jnp.zeros((8, hd), F32).at[0:4].set(
                jnp.stack([lam_q1[b], lam_k1[b], lam_q2[b], lam_k2[b]]).astype(F32))
            o = _diff_attention(q, kv, lam_vecs, attn_subln_g[b], batch=batch, heads=heads, hd=hd,
                                lambda_init=lambda_init)
            h = _matmul_norm_res(o, _to_bf16(attn_w_o, b), h, mix_post_g[l], bm=ROWS["attn_out_proj"], name="attn_out_proj")
        bf = min(MLP_HIDDEN_TILE, mlp_w_up.shape[2])
        h = _mlp(h, mlp_pre_g[l], _to_bf16(mlp_w_up, l, col_tile=bf), _to_bf16(mlp_w_down, l), mlp_post_g[l],
                 bm=ROWS["mlp"])
        if l == n_a - 1:
            kv = _norm_matmul(h, kv_norm_g, _to_bf16(w_kv), out_dtype=BF16, bm=ROWS["kv_proj"], name="kv_proj",
                              rope=(cos, sin), rope_cols=d, scale=1.0)
    return h.reshape(batch, seq, d)
```

```python
import functools
import math

import jax
import jax.numpy as jnp
from jax import lax
from jax.experimental import pallas as pl
from jax.experimental.pallas import tpu as pltpu

EPS = 1e-6
ROPE_THETA = 10000.0
LANES = 128
SSM_TC = 8
SSM_SEQS = 4
VMEM_LIMIT_BYTES = 56 * 1024 * 1024
MLP_VMEM_LIMIT_BYTES = 58 * 1024 * 1024
MASK_VALUE = -1e30
MLP_DOWN_COLS = 512
ROW_SUB = 256
COL_SLAB = 512
CAST_BLOCK_ELEMS = 2 * 1024 * 1024
ROWS = {"ssm_in_proj": 1024, "ssm_glu_out": 512, "kv_proj": 1024, "q_proj": 1024, "attn_out_proj": 1024,
        "mlp": 1024}
MLP_HIDDEN_TILE = 1024
ATTN_SOFTMAX_LAG = 1
ATTN_PV_LAG = 2

F32 = jnp.float32
BF16 = jnp.bfloat16


def _params(*sem, vmem_limit_bytes=VMEM_LIMIT_BYTES):
    return pltpu.CompilerParams(dimension_semantics=sem, vmem_limit_bytes=vmem_limit_bytes)


def _rms(x, g):
    return x * lax.rsqrt(jnp.mean(x * x, axis=-1, keepdims=True) + EPS) * g


def _dot(a, b):
    return jnp.dot(a, b, preferred_element_type=F32)


def _rope(t, cos, sin_signed):
    return t * cos + pltpu.roll(t, LANES // 2, 1) * sin_signed


def _make_norm_matmul_kernel(rope_cols, scale):
    def kernel(x_ref, g_ref, w_ref, *rest):
        if rope_cols:
            cos_ref, sin_ref, o_ref = rest
        else:
            (o_ref,) = rest
        bm, n = o_ref.shape
        sub, slab = min(ROW_SUB, bm), min(COL_SLAB, n)
        g = g_ref[...]
        for r in range(bm // sub):
            rs = slice(r * sub, (r + 1) * sub)
            xn = _rms(x_ref[rs, :], g).astype(BF16)
            for c in range(n // slab):
                acc = _dot(xn, w_ref[:, c * slab:(c + 1) * slab])
                if c * slab < rope_cols:
                    cos, sin = cos_ref[rs, :], sin_ref[rs, :]
                    for cc in range(slab // LANES):
                        sl = slice(cc * LANES, (cc + 1) * LANES)
                        osl = slice(c * slab + cc * LANES, c * slab + (cc + 1) * LANES)
                        o_ref[rs, osl] = (_rope(acc[:, sl], cos, sin) * scale).astype(o_ref.dtype)
                else:
                    o_ref[rs, c * slab:(c + 1) * slab] = acc.astype(o_ref.dtype)

    return kernel


def _norm_matmul(x, g, w, *, out_dtype, bm, name, rope=None, rope_cols=0, scale=1.0):
    m, k = x.shape
    n = w.shape[1]
    bm = min(bm, m)
    args = [x, g.reshape(1, k), w]
    if rope is not None:
        cos, sin = rope
        seq = cos.shape[0]
        bm = min(bm, seq)
        assert rope_cols % min(COL_SLAB, n) == 0 and seq % bm == 0
        nseq = seq // bm
        args += [cos, sin]
    in_specs = [
        pl.BlockSpec((bm, k), lambda i: (i, 0)),
        pl.BlockSpec((1, k), lambda i: (0, 0)),
        pl.BlockSpec((k, n), lambda i: (0, 0), pipeline_mode=pl.Buffered(1)),
    ]
    if rope is not None:
        in_specs += [pl.BlockSpec((bm, LANES), lambda i: (i % nseq, 0))] * 2
    return pl.pallas_call(
        _make_norm_matmul_kernel(rope_cols, scale),
        grid=(m // bm,),
        in_specs=in_specs,
        out_specs=pl.BlockSpec((bm, n), lambda i: (i, 0)),
        out_shape=jax.ShapeDtypeStruct((m, n), out_dtype),
        compiler_params=_params("parallel"),
        name=name,
    )(*args)


def _ssm_prep_kernel(pw_ref, bb_ref, cc_ref, wi_ref, wn_ref, wo_ref, *, tc, p_ch, n_st):
    s_dim = pw_ref.shape[1]
    gpb = LANES // p_ch
    row_g = lax.shift_right_logical(lax.broadcasted_iota(jnp.int32, (LANES, s_dim), 0), int(math.log2(p_ch)))
    col_g = lax.shift_right_logical(lax.broadcasted_iota(jnp.int32, (LANES, s_dim), 1), int(math.log2(n_st)))
    same_group = row_g == col_g

    def expand(x):
        return jnp.where(same_group, jnp.concatenate([x] * gpb, axis=0), 0.0)

    b_r, b_i = expand(bb_ref[0]), expand(bb_ref[1])
    c_r, c_i = expand(cc_ref[0]), expand(cc_ref[1])
    def split(x):
        hi = x.astype(BF16)
        return hi, (x - hi.astype(F32)).astype(BF16)

    c0_hi, c0_lo = split(jnp.concatenate([c_r, -c_i], axis=1).T)
    kd = []
    for d in range(tc):
        pr, pi = pw_ref[d:d + 1, :], pw_ref[tc + 1 + d:tc + 2 + d, :]
        e_hi, e_lo = split(jnp.concatenate([pr * b_r - pi * b_i, pr * b_i + pi * b_r], axis=1))
        wn_ref[(tc - 1 - d) * LANES:(tc - d) * LANES, :] = e_hi
        kd.append((_dot(e_hi, c0_hi) + (_dot(e_lo, c0_hi) + _dot(e_hi, c0_lo))).astype(BF16))
    for t in range(tc):
        pr, pi = pw_ref[t + 1:t + 2, :], pw_ref[tc + 2 + t:tc + 3 + t, :]
        e = jnp.concatenate([pr * c_r - pi * c_i, -(pr * c_i + pi * c_r)], axis=1)
        wo_ref[:, t * LANES:(t + 1) * LANES] = e.T.astype(BF16)
    zero = jnp.zeros((LANES, LANES), BF16)
    for s in range(tc):
        for t in range(tc):
            wi_ref[s * LANES:(s + 1) * LANES, t * LANES:(t + 1) * LANES] = kd[t - s] if t >= s else zero


def _ssm_weights(a_re, a_im, log_dt, b_re, b_im, c_re, c_im, tc):
    g_n, n_st = a_re.shape
    p_ch = b_re.shape[-1]
    gpb = LANES // p_ch
    nj = g_n // gpb
    s_dim = gpb * n_st
    assert p_ch & (p_ch - 1) == 0 and n_st & (n_st - 1) == 0
    step = jnp.exp(log_dt.astype(F32))[:, None]
    lam_re = jnp.minimum(a_re.astype(F32), -1e-4)
    lam_im = a_im.astype(F32)
    mag = jnp.exp(step * lam_re)
    abar_re = mag * jnp.cos(step * lam_im)
    abar_im = mag * jnp.sin(step * lam_im)
    den = lam_re * lam_re + lam_im * lam_im
    nr = abar_re - 1.0
    ni = abar_im
    coef_re = (nr * lam_re + ni * lam_im) / den
    coef_im = (ni * lam_re - nr * lam_im) / den
    bre, bim = b_re.astype(F32), b_im.astype(F32)
    bbar_re = coef_re[..., None] * bre - coef_im[..., None] * bim
    bbar_im = coef_re[..., None] * bim + coef_im[..., None] * bre
    dd = jnp.arange(tc + 1, dtype=F32)[:, None, None]
    pmag = jnp.exp(dd * (step * lam_re))
    pw = jnp.concatenate([pmag * jnp.cos(dd * (step * lam_im)),
                          pmag * jnp.sin(dd * (step * lam_im))], axis=0)
    pw = pw.reshape(2 * (tc + 1), nj, s_dim).transpose(1, 0, 2)
    bb = jnp.stack([bbar_re, bbar_im]).reshape(2, nj, gpb, n_st, p_ch)
    bb = bb.transpose(1, 0, 4, 2, 3).reshape(nj, 2, p_ch, s_dim)
    cc = jnp.stack([c_re.astype(F32), c_im.astype(F32)]).reshape(2, nj, gpb, p_ch, n_st)
    cc = cc.transpose(1, 0, 3, 2, 4).reshape(nj, 2, p_ch, s_dim)
    k_dim = tc * LANES
    wshape = jax.ShapeDtypeStruct((nj, k_dim, k_dim), BF16)
    assert 2 * s_dim == k_dim
    w_intra, w_in, w_out = pl.pallas_call(
        functools.partial(_ssm_prep_kernel, tc=tc, p_ch=p_ch, n_st=n_st),
        grid=(nj,),
        in_specs=[
            pl.BlockSpec((None, 2 * (tc + 1), s_dim), lambda j: (j, 0, 0)),
            pl.BlockSpec((None, 2, p_ch, s_dim), lambda j: (j, 0, 0, 0)),
            pl.BlockSpec((None, 2, p_ch, s_dim), lambda j: (j, 0, 0, 0)),
        ],
        out_specs=[pl.BlockSpec((None, k_dim, k_dim), lambda j: (j, 0, 0))] * 3,
        out_shape=[wshape] * 3,
        compiler_params=_params("parallel"),
        name="ssm_prep",
    )(pw, bb, cc)
    a_step = jnp.stack([pw[:, tc], pw[:, 2 * tc + 1]], axis=1)
    return w_intra, w_in, w_out, a_step


def _gelu_tanh(x):
    return 0.5 * x * (1.0 + jnp.tanh(math.sqrt(2.0 / math.pi) * (x + 0.044715 * (x * x * x))))


def _ssm_kernel(u_ref, wi_ref, wn_ref, wo_ref, a_ref, d_ref, z_ref, st_ref, zs_ref, *, nb, nc):
    tc = SSM_TC
    rows = nb * nc
    nk = st_ref.shape[0] // 2

    def step_rows(s):
        return pl.ds(s, rows, stride=tc)

    lhs = jnp.concatenate([u_ref[step_rows(s), :].astype(BF16) for s in range(tc)], axis=1)
    contrib = _dot(lhs, wn_ref[...])
    for k in range(2 * nk):
        for b in range(nb):
            st_ref[k, pl.ds(b, nc, stride=nb), :] = contrib[b * nc:(b + 1) * nc, k * LANES:(k + 1) * LANES]

    a = a_ref[...]
    ar = [jnp.broadcast_to(a[0:1, k * LANES:(k + 1) * LANES], (nb, LANES)) for k in range(nk)]
    ai = [jnp.broadcast_to(a[1:2, k * LANES:(k + 1) * LANES], (nb, LANES)) for k in range(nk)]

    def step(c, carry):
        rws = pl.ds(pl.multiple_of(c * nb, nb), nb)
        out = []
        for k in range(nk):
            sr, si = carry[2 * k], carry[2 * k + 1]
            cr = st_ref[k, rws, :]
            ci = st_ref[nk + k, rws, :]
            st_ref[k, rws, :] = sr
            st_ref[nk + k, rws, :] = si
            out += [ar[k] * sr - ai[k] * si + cr, ar[k] * si + ai[k] * sr + ci]
        return tuple(out)

    y_intra = _dot(lhs, wi_ref[...])
    zero = jnp.zeros((nb, LANES), F32)
    lax.fori_loop(0, nc, step, (zero,) * (2 * nk), unroll=True)

    state = jnp.concatenate(
        [jnp.concatenate([st_ref[k, pl.ds(b, nc, stride=nb), :] for b in range(nb)], axis=0).astype(BF16)
         for k in range(2 * nk)], axis=1)
    y = y_intra + _dot(state, wo_ref[...])
    d = d_ref[...]
    for t in range(tc):
        yt = y[:, t * LANES:(t + 1) * LANES] + d * u_ref[step_rows(t), :]
        zs_ref[step_rows(t), :] = _gelu_tanh(yt)
    z_ref[...] = zs_ref[...].astype(z_ref.dtype)


def _ssm_scan(u, w_intra, w_in, w_out, a_step, d_skip, *, batch, side_jobs=()):
    t, d = u.shape
    seq = t // batch
    tc = SSM_TC
    nc = seq // tc
    nb = min(batch, SSM_SEQS)
    k_dim = tc * LANES
    return _host_call(
        functools.partial(_ssm_kernel, nb=nb, nc=nc),
        (u, w_intra, w_in, w_out, a_step, d_skip.reshape(1, d)),
        grid=(d // LANES, batch // nb),
        in_specs=[
            pl.BlockSpec((nb * seq, LANES), lambda j, b: (b, j)),
            pl.BlockSpec((None, k_dim, k_dim), lambda j, b: (j, 0, 0)),
            pl.BlockSpec((None, k_dim, k_dim), lambda j, b: (j, 0, 0)),
            pl.BlockSpec((None, k_dim, k_dim), lambda j, b: (j, 0, 0)),
            pl.BlockSpec((None, 2, k_dim // 2), lambda j, b: (j, 0, 0)),
            pl.BlockSpec((1, LANES), lambda j, b: (0, j)),
        ],
        out_spec=pl.BlockSpec((nb * seq, LANES), lambda j, b: (b, j)),
        out_shape=jax.ShapeDtypeStruct((t, d), BF16),
        side_jobs=side_jobs,
        scratch_shapes=[pltpu.VMEM((k_dim // LANES, nc * nb, LANES), F32),
                        pltpu.VMEM((nb * seq, LANES), F32)],
        compiler_params=_params("parallel", "parallel"),
        name="ssm_scan",
    )


def _make_matmul_norm_res_kernel(glu):
    def kernel(a_ref, w_ref, x_ref, g_ref, o_ref):
        bm, d = o_ref.shape
        sub, slab = min(ROW_SUB, bm), min(COL_SLAB, d)
        g = g_ref[...]
        for r in range(bm // sub):
            rs = slice(r * sub, (r + 1) * sub)
            a = a_ref[rs, :]
            ssq = jnp.zeros((sub, 1), F32)
            for c in range(d // slab):
                cs = slice(c * slab, (c + 1) * slab)
                mix = _dot(a, w_ref[:, cs])
                if glu:
                    gate = _dot(a, w_ref[:, d + c * slab:d + (c + 1) * slab])
                    mix = mix * (1.0 / (1.0 + jnp.exp(-gate)))
                o_ref[rs, cs] = mix
                ssq = ssq + jnp.sum(mix * mix, axis=1, keepdims=True)
            o_ref[rs, :] = x_ref[rs, :] + o_ref[rs, :] * lax.rsqrt(ssq * (1.0 / d) + EPS) * g

    return kernel


def _matmul_norm_res(a, w, x, g, *, bm, name, glu=False, side_jobs=()):
    m, k = a.shape
    d = x.shape[1]
    bm = min(bm, m)
    return _host_call(
        _make_matmul_norm_res_kernel(glu),
        (a, w, x, g.reshape(1, d)),
        grid=(m // bm,),
        in_specs=[
            pl.BlockSpec((bm, k), lambda i: (i, 0)),
            pl.BlockSpec(w.shape, lambda i: (0, 0), pipeline_mode=pl.Buffered(1)),
            pl.BlockSpec((bm, d), lambda i: (i, 0)),
            pl.BlockSpec((1, d), lambda i: (0, 0)),
        ],
        out_spec=pl.BlockSpec((bm, d), lambda i: (i, 0)),
        out_shape=jax.ShapeDtypeStruct((m, d), F32),
        side_jobs=side_jobs,
        compiler_params=_params("parallel"),
        name=name,
    )


def _mlp_kernel(x_ref, gpre_ref, wu_ref, wd_ref, gpost_ref, o_ref, xn_ref, *, nf):
    f = pl.program_id(1)
    bm, d = o_ref.shape
    bn = min(MLP_DOWN_COLS, d)

    def up_down(xn, rs, first):
        a = jnp.square(jnp.maximum(_dot(xn, wu_ref[...]), 0.0)).astype(BF16)
        for c in range(d // bn):
            sl = slice(c * bn, (c + 1) * bn)
            part = _dot(a, wd_ref[:, sl])
            if first:
                o_ref[rs, sl] = part
            else:
                o_ref[rs, sl] += part

    def edge_step(first, last):
        sub = min(ROW_SUB, bm)
        for r in range(bm // sub):
            rs = slice(r * sub, (r + 1) * sub)
            if first:
                xn = _rms(x_ref[rs, :], gpre_ref[...]).astype(BF16)
                xn_ref[rs, :] = xn
            else:
                xn = xn_ref[rs, :]
            up_down(xn, rs, first)
            if last:
                o_ref[rs, :] = x_ref[rs, :] + _rms(o_ref[rs, :], gpost_ref[...])

    @pl.when(f == 0)
    def _():
        edge_step(True, nf == 1)

    if nf > 2:
        @pl.when(jnp.logical_and(f > 0, f < nf - 1))
        def _():
            up_down(xn_ref[...], slice(None), False)

    if nf > 1:
        @pl.when(f == nf - 1)
        def _():
            edge_step(False, True)


def _mlp(x, g_pre, w_up, w_down, g_post, *, bm):
    m, d = x.shape
    nf, _, bf = w_up.shape
    dff = nf * bf
    bm = min(bm, m)
    return pl.pallas_call(
        functools.partial(_mlp_kernel, nf=dff // bf),
        grid=(m // bm, dff // bf),
        in_specs=[
            pl.BlockSpec((bm, d), lambda i, f: (i, 0), pipeline_mode=pl.Buffered(1)),
            pl.BlockSpec((1, d), lambda i, f: (0, 0)),
            pl.BlockSpec((None, d, bf), lambda i, f: (f, 0, 0)),
            pl.BlockSpec((bf, d), lambda i, f: (f, 0)),
            pl.BlockSpec((1, d), lambda i, f: (0, 0)),
        ],
        out_specs=pl.BlockSpec((bm, d), lambda i, f: (i, 0)),
        out_shape=jax.ShapeDtypeStruct((m, d), F32),
        scratch_shapes=[pltpu.VMEM((bm, d), BF16)],
        compiler_params=_params("parallel", "arbitrary", vmem_limit_bytes=MLP_VMEM_LIMIT_BYTES),
        name="mlp",
    )(x, g_pre.reshape(1, d), w_up, w_down, g_post.reshape(1, d))


def _attn_kernel(q_ref, k_ref, v_ref, lam_ref, g_ref, o_ref, *, hd, lambda_init):
    seq = q_ref.shape[0]
    sub = min(ROW_SUB, seq)
    nt = (((1,), (1,)), ((), ()))
    lv = lam_ref[...]
    lam = (jnp.exp(jnp.sum(lv[0:1] * lv[1:2], axis=1, keepdims=True))
           - jnp.exp(jnp.sum(lv[2:3] * lv[3:4], axis=1, keepdims=True)) + lambda_init)
    row = lax.broadcasted_iota(jnp.int32, (sub, sub), 0)
    col = lax.broadcasted_iota(jnp.int32, (sub, sub), 1)
    causal = col <= row

    chains = [(r0, c0) for r0 in reversed(range(0, seq, sub)) for c0 in (0, hd)]

    def scores(i):
        r0, c0 = chains[i]
        kv_len = r0 + sub
        s = lax.dot_general(q_ref[r0:r0 + sub, c0:c0 + hd], k_ref[0:kv_len, c0:c0 + hd], nt,
                            preferred_element_type=F32)
        s_diag = jnp.where(causal, s[:, kv_len - sub:], MASK_VALUE)
        return s_diag if kv_len == sub else jnp.concatenate([s[:, :kv_len - sub], s_diag], axis=1)

    def softmax(s):
        p = jnp.exp(s - jnp.max(s, axis=1, keepdims=True))
        return p.astype(BF16), jnp.sum(p, axis=1, keepdims=True)

    def weighted_v(i, p, l):
        return _dot(p, v_ref[0:chains[i][0] + sub, :]) * (1.0 / l)

    s_vals, p_vals, o_vals = {}, {}, {}
    for t in range(len(chains) + ATTN_PV_LAG):
        if t < len(chains):
            s_vals[t] = scores(t)
        if 0 <= t - ATTN_SOFTMAX_LAG < len(chains):
            p_vals[t - ATTN_SOFTMAX_LAG] = softmax(s_vals.pop(t - ATTN_SOFTMAX_LAG))
        if 0 <= t - ATTN_PV_LAG < len(chains):
            i = t - ATTN_PV_LAG
            o_vals[i] = weighted_v(i, *p_vals.pop(i))
            if i % 2 == 1:
                r0 = chains[i][0]
                o = o_vals.pop(i - 1) - lam * o_vals.pop(i)
                o_ref[r0:r0 + sub, :] = (_rms(o, g_ref[...]) * (1.0 - lambda_init)).astype(o_ref.dtype)


def _diff_attention(q, kv, lam_vecs, g_sub, *, batch, heads, hd, lambda_init, side_jobs=()):
    t, d = q.shape
    seq = t // batch
    head_block = pl.BlockSpec((seq, 2 * hd), lambda b, h: (b, h))
    return _host_call(
        functools.partial(_attn_kernel, hd=hd, lambda_init=lambda_init),
        (q, kv, kv, lam_vecs, g_sub.reshape(1, 2 * hd)),
        grid=(batch, heads),
        in_specs=[
            head_block,
            head_block,
            pl.BlockSpec((seq, 2 * hd), lambda b, h: (b, heads + h)),
            pl.BlockSpec((8, hd), lambda b, h: (0, 0)),
            pl.BlockSpec((1, 2 * hd), lambda b, h: (0, 0)),
        ],
        out_spec=head_block,
        out_shape=jax.ShapeDtypeStruct((t, d), BF16),
        side_jobs=side_jobs,
        compiler_params=_params("parallel", "parallel"),
        name="diff_attn",
    )


def _cast_kernel(w_ref, o_ref):
    if len(o_ref.shape) == 2:
        o_ref[...] = w_ref[...].astype(o_ref.dtype)
    else:
        tile = o_ref.shape[2]
        for c in range(o_ref.shape[0]):
            o_ref[c] = w_ref[:, c * tile:(c + 1) * tile].astype(o_ref.dtype)


def _to_bf16(w, layer=None, col_tile=None):
    if layer is None:
        w, layer = w[None], 0
    _, k, n = w.shape
    rows = max(8, min(k, CAST_BLOCK_ELEMS // n))
    assert k % rows == 0
    if col_tile is None:
        out_spec = pl.BlockSpec((rows, n), lambda i: (i, 0))
        out_shape = jax.ShapeDtypeStruct((k, n), BF16)
    else:
        out_spec = pl.BlockSpec((n // col_tile, rows, col_tile), lambda i: (0, i, 0))
        out_shape = jax.ShapeDtypeStruct((n // col_tile, k, col_tile), BF16)
    return pl.pallas_call(
        _cast_kernel,
        grid=(k // rows,),
        in_specs=[pl.BlockSpec((None, rows, n), lambda i: (layer, i, 0))],
        out_specs=out_spec,
        out_shape=out_shape,
        compiler_params=_params("parallel"),
        name="cast_bf16",
    )(w)


def _side_casts(jobs, grid):
    steps = math.prod(grid)

    def flat(*ids):
        i = ids[0]
        for extent, idx in zip(grid[1:], ids[1:]):
            i = i * extent + idx
        return i

    args, in_specs, out_specs, out_shapes = [], [], [], []
    for w, layer, col_tile in jobs:
        _, k, n = w.shape
        rows = k // steps
        if k % steps or rows % 16:
            return None
        args.append(w)
        in_specs.append(pl.BlockSpec((None, rows, n), lambda *ids, layer=layer: (layer, flat(*ids), 0)))
        if col_tile is None:
            out_specs.append(pl.BlockSpec((rows, n), lambda *ids: (flat(*ids), 0)))
            out_shapes.append(jax.ShapeDtypeStruct((k, n), BF16))
        else:
            out_specs.append(pl.BlockSpec((n // col_tile, rows, col_tile), lambda *ids: (0, flat(*ids), 0)))
            out_shapes.append(jax.ShapeDtypeStruct((n // col_tile, k, col_tile), BF16))
    return args, in_specs, out_specs, out_shapes


def _with_side_casts(kernel, n_in, n_side):
    def wrapped(*refs):
        main_in, side_in = refs[:n_in], refs[n_in:n_in + n_side]
        out = refs[n_in + n_side]
        side_out = refs[n_in + n_side + 1:n_in + 2 * n_side + 1]
        for w_ref, o_ref in zip(side_in, side_out):
            _cast_kernel(w_ref, o_ref)
        kernel(*main_in, out, *refs[n_in + 2 * n_side + 1:])

    return wrapped


def _host_call(kernel, args, *, grid, in_specs, out_spec, out_shape, side_jobs=(), **kw):
    side = _side_casts(side_jobs, grid) if side_jobs else None
    if side is None:
        out = pl.pallas_call(kernel, grid=grid, in_specs=in_specs, out_specs=out_spec, out_shape=out_shape,
                             **kw)(*args)
        return out, [_to_bf16(w, layer, col_tile) for w, layer, col_tile in side_jobs]
    s_args, s_in, s_out, s_shapes = side
    outs = pl.pallas_call(
        _with_side_casts(kernel, len(args), len(s_args)), grid=grid, in_specs=list(in_specs) + s_in,
        out_specs=[out_spec] + s_out, out_shape=[out_shape] + s_shapes, **kw)(*args, *s_args)
    return outs[0], list(outs[1:])


def _rope_tables(seq, hd):
    pos = jnp.arange(seq, dtype=F32)
    inv_freq = 1.0 / (ROPE_THETA ** (jnp.arange(0, hd, 2, dtype=F32) / hd))
    ang = pos[:, None] * inv_freq[None, :]
    emb = jnp.concatenate([ang, ang], axis=-1)
    sign = jnp.where(jnp.arange(hd) < hd // 2, -1.0, 1.0).astype(F32)
    return jnp.cos(emb), jnp.sin(emb) * sign


def kernel(x, mix_pre_g, mix_post_g, mlp_pre_g, mlp_post_g, ssm_w_in, ssm_a_re, ssm_a_im, ssm_log_dt, ssm_b_re, ssm_b_im, ssm_c_re, ssm_c_im, ssm_d, ssm_w_glu, kv_norm_g, w_kv, attn_w_q, lam_q1, lam_k1, lam_q2, lam_k2, attn_subln_g, attn_w_o, mlp_w_up, mlp_w_down):
    batch, seq, d = x.shape
    t = batch * seq
    depth = mix_pre_g.shape[0]
    n_a = ssm_w_in.shape[0]
    hd = lam_q1.shape[1]
    heads = d // (2 * hd)
    cos, sin = _rope_tables(seq, hd)

    h = x.reshape(t, d)
    kv = None
    bf = min(MLP_HIDDEN_TILE, mlp_w_up.shape[2])
    for l in range(depth):
        up_job, down_job = (mlp_w_up, l, bf), (mlp_w_down, l, None)
        if l < n_a:
            a = l
            u = _norm_matmul(h, mix_pre_g[l], _to_bf16(ssm_w_in, a), out_dtype=F32, bm=ROWS["ssm_in_proj"], name="ssm_in_proj")
            sw = _ssm_weights(ssm_a_re[a], ssm_a_im[a], ssm_log_dt[a], ssm_b_re[a], ssm_b_im[a],
                              ssm_c_re[a], ssm_c_im[a], SSM_TC)
            z, (w_up,) = _ssm_scan(u, *sw, ssm_d[a], batch=batch, side_jobs=(up_job,))
            h, (w_down,) = _matmul_norm_res(z, _to_bf16(ssm_w_glu, a), h, mix_post_g[l], bm=ROWS["ssm_glu_out"],
                                            glu=True, name="ssm_glu_out", side_jobs=(down_job,))
        else:
            b = l - n_a
            lambda_init = 0.8 - 0.6 * math.exp(-0.3 * l)
            q = _norm_matmul(h, mix_pre_g[l], _to_bf16(attn_w_q, b), out_dtype=BF16, bm=ROWS["q_proj"], name="q_proj",
                             rope=(cos, sin), rope_cols=d, scale=hd ** -0.5)
            lam_vecs = jnp.zeros((8, hd), F32).at[0:4].set(
                jnp.stack([lam_q1[b], lam_k1[b], lam_q2[b], lam_k2[b]]).astype(F32))
            o, (w_up, w_down) = _diff_attention(q, kv, lam_vecs, attn_subln_g[b], batch=batch, heads=heads, hd=hd,
                                                lambda_init=lambda_init, side_jobs=(up_job, down_job))
            h, _ = _matmul_norm_res(o, _to_bf16(attn_w_o, b), h, mix_post_g[l], bm=ROWS["attn_out_proj"],
                                    name="attn_out_proj")
        h = _mlp(h, mlp_pre_g[l], w_up, w_down, mlp_post_g[l], bm=ROWS["mlp"])
        if l == n_a - 1:
            kv = _norm_matmul(h, kv_norm_g, _to_bf16(w_kv), out_dtype=BF16, bm=ROWS["kv_proj"], name="kv_proj",
                              rope=(cos, sin), rope_cols=d, scale=1.0)
    return h.reshape(batch, seq, d)
```

```python
import functools
import math

import jax
import jax.numpy as jnp
from jax import lax
from jax.experimental import pallas as pl
from jax.experimental.pallas import tpu as pltpu

EPS = 1e-6
ROPE_THETA = 10000.0
LANES = 128
SSM_TC = 8
SSM_SEQS = 4
VMEM_LIMIT_BYTES = 56 * 1024 * 1024
MLP_VMEM_LIMIT_BYTES = 58 * 1024 * 1024
MASK_VALUE = -1e30
MLP_DOWN_COLS = 512
ROW_SUB = 256
COL_SLAB = 512
CAST_BLOCK_ELEMS = 2 * 1024 * 1024
ROWS = {"ssm_in_proj": 1024, "ssm_glu_out": 512, "kv_proj": 1024, "q_proj": 1024, "attn_out_proj": 1024,
        "mlp": 1024}
MLP_HIDDEN_TILE = 1024
ATTN_SOFTMAX_LAG = 1
ATTN_PV_LAG = 2

F32 = jnp.float32
BF16 = jnp.bfloat16


def _params(*sem, vmem_limit_bytes=VMEM_LIMIT_BYTES):
    return pltpu.CompilerParams(dimension_semantics=sem, vmem_limit_bytes=vmem_limit_bytes)


def _rms(x, g):
    return x * lax.rsqrt(jnp.mean(x * x, axis=-1, keepdims=True) + EPS) * g


def _dot(a, b):
    return jnp.dot(a, b, preferred_element_type=F32)


def _rope(t, cos, sin_signed):
    return t * cos + pltpu.roll(t, LANES // 2, 1) * sin_signed


def _make_norm_matmul_kernel(rope_cols, scale):
    def kernel(x_ref, g_ref, w_ref, *rest):
        if rope_cols:
            cos_ref, sin_ref, o_ref = rest
        else:
            (o_ref,) = rest
        bm, n = o_ref.shape
        sub, slab = min(ROW_SUB, bm), min(COL_SLAB, n)
        g = g_ref[...]
        for r in range(bm // sub):
            rs = slice(r * sub, (r + 1) * sub)
            xn = _rms(x_ref[rs, :], g).astype(BF16)
            for c in range(n // slab):
                acc = _dot(xn, w_ref[:, c * slab:(c + 1) * slab])
                if c * slab < rope_cols:
                    cos, sin = cos_ref[rs, :], sin_ref[rs, :]
                    for cc in range(slab // LANES):
                        sl = slice(cc * LANES, (cc + 1) * LANES)
                        osl = slice(c * slab + cc * LANES, c * slab + (cc + 1) * LANES)
                        o_ref[rs, osl] = (_rope(acc[:, sl], cos, sin) * scale).astype(o_ref.dtype)
                else:
                    o_ref[rs, c * slab:(c + 1) * slab] = acc.astype(o_ref.dtype)

    return kernel


def _norm_matmul(x, g, w, *, out_dtype, bm, name, rope=None, rope_cols=0, scale=1.0, side_jobs=()):
    m, k = x.shape
    n = w.shape[1]
    bm = min(bm, m)
    args = [x, g.reshape(1, k), w]
    if rope is not None:
        cos, sin = rope
        seq = cos.shape[0]
        bm = min(bm, seq)
        assert rope_cols % min(COL_SLAB, n) == 0 and seq % bm == 0
        nseq = seq // bm
        args += [cos, sin]
    in_specs = [
        pl.BlockSpec((bm, k), lambda i: (i, 0)),
        pl.BlockSpec((1, k), lambda i: (0, 0)),
        pl.BlockSpec((k, n), lambda i: (0, 0), pipeline_mode=pl.Buffered(1)),
    ]
    if rope is not None:
        in_specs += [pl.BlockSpec((bm, LANES), lambda i: (i % nseq, 0))] * 2
    return _host_call(
        _make_norm_matmul_kernel(rope_cols, scale),
        args,
        grid=(m // bm,),
        in_specs=in_specs,
        out_spec=pl.BlockSpec((bm, n), lambda i: (i, 0)),
        out_shape=jax.ShapeDtypeStruct((m, n), out_dtype),
        side_jobs=side_jobs,
        compiler_params=_params("parallel"),
        name=name,
    )


def _ssm_prep_kernel(pw_ref, bb_ref, cc_ref, wi_ref, wn_ref, wo_ref, *, tc, p_ch, n_st):
    s_dim = pw_ref.shape[1]
    gpb = LANES // p_ch
    row_g = lax.shift_right_logical(lax.broadcasted_iota(jnp.int32, (LANES, s_dim), 0), int(math.log2(p_ch)))
    col_g = lax.shift_right_logical(lax.broadcasted_iota(jnp.int32, (LANES, s_dim), 1), int(math.log2(n_st)))
    same_group = row_g == col_g

    def expand(x):
        return jnp.where(same_group, jnp.concatenate([x] * gpb, axis=0), 0.0)

    b_r, b_i = expand(bb_ref[0]), expand(bb_ref[1])
    c_r, c_i = expand(cc_ref[0]), expand(cc_ref[1])
    def split(x):
        hi = x.astype(BF16)
        return hi, (x - hi.astype(F32)).astype(BF16)

    c0_hi, c0_lo = split(jnp.concatenate([c_r, -c_i], axis=1).T)
    kd = []
    for d in range(tc):
        pr, pi = pw_ref[d:d + 1, :], pw_ref[tc + 1 + d:tc + 2 + d, :]
        e_hi, e_lo = split(jnp.concatenate([pr * b_r - pi * b_i, pr * b_i + pi * b_r], axis=1))
        wn_ref[(tc - 1 - d) * LANES:(tc - d) * LANES, :] = e_hi
        kd.append((_dot(e_hi, c0_hi) + (_dot(e_lo, c0_hi) + _dot(e_hi, c0_lo))).astype(BF16))
    for t in range(tc):
        pr, pi = pw_ref[t + 1:t + 2, :], pw_ref[tc + 2 + t:tc + 3 + t, :]
        e = jnp.concatenate([pr * c_r - pi * c_i, -(pr * c_i + pi * c_r)], axis=1)
        wo_ref[:, t * LANES:(t + 1) * LANES] = e.T.astype(BF16)
    zero = jnp.zeros((LANES, LANES), BF16)
    for s in range(tc):
        for t in range(tc):
            wi_ref[s * LANES:(s + 1) * LANES, t * LANES:(t + 1) * LANES] = kd[t - s] if t >= s else zero


def _ssm_weights(a_re, a_im, log_dt, b_re, b_im, c_re, c_im, tc):
    g_n, n_st = a_re.shape
    p_ch = b_re.shape[-1]
    gpb = LANES // p_ch
    nj = g_n // gpb
    s_dim = gpb * n_st
    assert p_ch & (p_ch - 1) == 0 and n_st & (n_st - 1) == 0
    step = jnp.exp(log_dt.astype(F32))[:, None]
    lam_re = jnp.minimum(a_re.astype(F32), -1e-4)
    lam_im = a_im.astype(F32)
    mag = jnp.exp(step * lam_re)
    abar_re = mag * jnp.cos(step * lam_im)
    abar_im = mag * jnp.sin(step * lam_im)
    den = lam_re * lam_re + lam_im * lam_im
    nr = abar_re - 1.0
    ni = abar_im
    coef_re = (nr * lam_re + ni * lam_im) / den
    coef_im = (ni * lam_re - nr * lam_im) / den
    bre, bim = b_re.astype(F32), b_im.astype(F32)
    bbar_re = coef_re[..., None] * bre - coef_im[..., None] * bim
    bbar_im = coef_re[..., None] * bim + coef_im[..., None] * bre
    dd = jnp.arange(tc + 1, dtype=F32)[:, None, None]
    pmag = jnp.exp(dd * (step * lam_re))
    pw = jnp.concatenate([pmag * jnp.cos(dd * (step * lam_im)),
                          pmag * jnp.sin(dd * (step * lam_im))], axis=0)
    pw = pw.reshape(2 * (tc + 1), nj, s_dim).transpose(1, 0, 2)
    bb = jnp.stack([bbar_re, bbar_im]).reshape(2, nj, gpb, n_st, p_ch)
    bb = bb.transpose(1, 0, 4, 2, 3).reshape(nj, 2, p_ch, s_dim)
    cc = jnp.stack([c_re.astype(F32), c_im.astype(F32)]).reshape(2, nj, gpb, p_ch, n_st)
    cc = cc.transpose(1, 0, 3, 2, 4).reshape(nj, 2, p_ch, s_dim)
    k_dim = tc * LANES
    wshape = jax.ShapeDtypeStruct((nj, k_dim, k_dim), BF16)
    assert 2 * s_dim == k_dim
    w_intra, w_in, w_out = pl.pallas_call(
        functools.partial(_ssm_prep_kernel, tc=tc, p_ch=p_ch, n_st=n_st),
        grid=(nj,),
        in_specs=[
            pl.BlockSpec((None, 2 * (tc + 1), s_dim), lambda j: (j, 0, 0)),
            pl.BlockSpec((None, 2, p_ch, s_dim), lambda j: (j, 0, 0, 0)),
            pl.BlockSpec((None, 2, p_ch, s_dim), lambda j: (j, 0, 0, 0)),
        ],
        out_specs=[pl.BlockSpec((None, k_dim, k_dim), lambda j: (j, 0, 0))] * 3,
        out_shape=[wshape] * 3,
        compiler_params=_params("parallel"),
        name="ssm_prep",
    )(pw, bb, cc)
    a_step = jnp.stack([pw[:, tc], pw[:, 2 * tc + 1]], axis=1)
    return w_intra, w_in, w_out, a_step


def _gelu_tanh(x):
    return 0.5 * x * (1.0 + jnp.tanh(math.sqrt(2.0 / math.pi) * (x + 0.044715 * (x * x * x))))


def _ssm_kernel(u_ref, wi_ref, wn_ref, wo_ref, a_ref, d_ref, z_ref, st_ref, zs_ref, *, nb, nc):
    tc = SSM_TC
    rows = nb * nc
    nk = st_ref.shape[0] // 2

    def step_rows(s):
        return pl.ds(s, rows, stride=tc)

    lhs = jnp.concatenate([u_ref[step_rows(s), :].astype(BF16) for s in range(tc)], axis=1)
    contrib = _dot(lhs, wn_ref[...])
    for k in range(2 * nk):
        for b in range(nb):
            st_ref[k, pl.ds(b, nc, stride=nb), :] = contrib[b * nc:(b + 1) * nc, k * LANES:(k + 1) * LANES]

    a = a_ref[...]
    ar = [jnp.broadcast_to(a[0:1, k * LANES:(k + 1) * LANES], (nb, LANES)) for k in range(nk)]
    ai = [jnp.broadcast_to(a[1:2, k * LANES:(k + 1) * LANES], (nb, LANES)) for k in range(nk)]

    def step(c, carry):
        rws = pl.ds(pl.multiple_of(c * nb, nb), nb)
        out = []
        for k in range(nk):
            sr, si = carry[2 * k], carry[2 * k + 1]
            cr = st_ref[k, rws, :]
            ci = st_ref[nk + k, rws, :]
            st_ref[k, rws, :] = sr
            st_ref[nk + k, rws, :] = si
            out += [ar[k] * sr - ai[k] * si + cr, ar[k] * si + ai[k] * sr + ci]
        return tuple(out)

    y_intra = _dot(lhs, wi_ref[...])
    zero = jnp.zeros((nb, LANES), F32)
    lax.fori_loop(0, nc, step, (zero,) * (2 * nk), unroll=True)

    state = jnp.concatenate(
        [jnp.concatenate([st_ref[k, pl.ds(b, nc, stride=nb), :] for b in range(nb)], axis=0).astype(BF16)
         for k in range(2 * nk)], axis=1)
    y = y_intra + _dot(state, wo_ref[...])
    d = d_ref[...]
    for t in range(tc):
        yt = y[:, t * LANES:(t + 1) * LANES] + d * u_ref[step_rows(t), :]
        zs_ref[step_rows(t), :] = _gelu_tanh(yt)
    z_ref[...] = zs_ref[...].astype(z_ref.dtype)


def _ssm_scan(u, w_intra, w_in, w_out, a_step, d_skip, *, batch, side_jobs=()):
    t, d = u.shape
    seq = t // batch
    tc = SSM_TC
    nc = seq // tc
    nb = min(batch, SSM_SEQS)
    k_dim = tc * LANES
    return _host_call(
        functools.partial(_ssm_kernel, nb=nb, nc=nc),
        (u, w_intra, w_in, w_out, a_step, d_skip.reshape(1, d)),
        grid=(d // LANES, batch // nb),
        in_specs=[
            pl.BlockSpec((nb * seq, LANES), lambda j, b: (b, j)),
            pl.BlockSpec((None, k_dim, k_dim), lambda j, b: (j, 0, 0)),
            pl.BlockSpec((None, k_dim, k_dim), lambda j, b: (j, 0, 0)),
            pl.BlockSpec((None, k_dim, k_dim), lambda j, b: (j, 0, 0)),
            pl.BlockSpec((None, 2, k_dim // 2), lambda j, b: (j, 0, 0)),
            pl.BlockSpec((1, LANES), lambda j, b: (0, j)),
        ],
        out_spec=pl.BlockSpec((nb * seq, LANES), lambda j, b: (b, j)),
        out_shape=jax.ShapeDtypeStruct((t, d), BF16),
        side_jobs=side_jobs,
        scratch_shapes=[pltpu.VMEM((k_dim // LANES, nc * nb, LANES), F32),
                        pltpu.VMEM((nb * seq, LANES), F32)],
        compiler_params=_params("parallel", "parallel"),
        name="ssm_scan",
    )


def _make_matmul_norm_res_kernel(glu):
    def kernel(a_ref, w_ref, x_ref, g_ref, o_ref):
        bm, d = o_ref.shape
        sub, slab = min(ROW_SUB, bm), min(COL_SLAB, d)
        g = g_ref[...]
        for r in range(bm // sub):
            rs = slice(r * sub, (r + 1) * sub)
            a = a_ref[rs, :]
            ssq = jnp.zeros((sub, 1), F32)
            for c in range(d // slab):
                cs = slice(c * slab, (c + 1) * slab)
                mix = _dot(a, w_ref[:, cs])
                if glu:
                    gate = _dot(a, w_ref[:, d + c * slab:d + (c + 1) * slab])
                    mix = mix * (1.0 / (1.0 + jnp.exp(-gate)))
                o_ref[rs, cs] = mix
                ssq = ssq + jnp.sum(mix * mix, axis=1, keepdims=True)
            o_ref[rs, :] = x_ref[rs, :] + o_ref[rs, :] * lax.rsqrt(ssq * (1.0 / d) + EPS) * g

    return kernel


def _matmul_norm_res(a, w, x, g, *, bm, name, glu=False, side_jobs=()):
    m, k = a.shape
    d = x.shape[1]
    bm = min(bm, m)
    return _host_call(
        _make_matmul_norm_res_kernel(glu),
        (a, w, x, g.reshape(1, d)),
        grid=(m // bm,),
        in_specs=[
            pl.BlockSpec((bm, k), lambda i: (i, 0)),
            pl.BlockSpec(w.shape, lambda i: (0, 0), pipeline_mode=pl.Buffered(1)),
            pl.BlockSpec((bm, d), lambda i: (i, 0)),
            pl.BlockSpec((1, d), lambda i: (0, 0)),
        ],
        out_spec=pl.BlockSpec((bm, d), lambda i: (i, 0)),
        out_shape=jax.ShapeDtypeStruct((m, d), F32),
        side_jobs=side_jobs,
        compiler_params=_params("parallel"),
        name=name,
    )


def _mlp_kernel(x_ref, gpre_ref, wu_ref, wd_ref, gpost_ref, o_ref, xn_ref, *, nf):
    f = pl.program_id(1)
    bm, d = o_ref.shape
    bn = min(MLP_DOWN_COLS, d)

    def up_down(xn, rs, first):
        a = jnp.square(jnp.maximum(_dot(xn, wu_ref[...]), 0.0)).astype(BF16)
        for c in range(d // bn):
            sl = slice(c * bn, (c + 1) * bn)
            part = _dot(a, wd_ref[:, sl])
            if first:
                o_ref[rs, sl] = part
            else:
                o_ref[rs, sl] += part

    def edge_step(first, last):
        sub = min(ROW_SUB, bm)
        for r in range(bm // sub):
            rs = slice(r * sub, (r + 1) * sub)
            if first:
                xn = _rms(x_ref[rs, :], gpre_ref[...]).astype(BF16)
                xn_ref[rs, :] = xn
            else:
                xn = xn_ref[rs, :]
            up_down(xn, rs, first)
            if last:
                o_ref[rs, :] = x_ref[rs, :] + _rms(o_ref[rs, :], gpost_ref[...])

    @pl.when(f == 0)
    def _():
        edge_step(True, nf == 1)

    if nf > 2:
        @pl.when(jnp.logical_and(f > 0, f < nf - 1))
        def _():
            up_down(xn_ref[...], slice(None), False)

    if nf > 1:
        @pl.when(f == nf - 1)
        def _():
            edge_step(False, True)


def _mlp(x, g_pre, w_up, w_down, g_post, *, bm):
    m, d = x.shape
    nf, _, bf = w_up.shape
    dff = nf * bf
    bm = min(bm, m)
    return pl.pallas_call(
        functools.partial(_mlp_kernel, nf=dff // bf),
        grid=(m // bm, dff // bf),
        in_specs=[
            pl.BlockSpec((bm, d), lambda i, f: (i, 0), pipeline_mode=pl.Buffered(1)),
            pl.BlockSpec((1, d), lambda i, f: (0, 0)),
            pl.BlockSpec((None, d, bf), lambda i, f: (f, 0, 0)),
            pl.BlockSpec((bf, d), lambda i, f: (f, 0)),
            pl.BlockSpec((1, d), lambda i, f: (0, 0)),
        ],
        out_specs=pl.BlockSpec((bm, d), lambda i, f: (i, 0)),
        out_shape=jax.ShapeDtypeStruct((m, d), F32),
        scratch_shapes=[pltpu.VMEM((bm, d), BF16)],
        compiler_params=_params("parallel", "arbitrary", vmem_limit_bytes=MLP_VMEM_LIMIT_BYTES),
        name="mlp",
    )(x, g_pre.reshape(1, d), w_up, w_down, g_post.reshape(1, d))


def _attn_kernel(q_ref, k_ref, v_ref, lam_ref, g_ref, o_ref, *, hd, lambda_init):
    seq = q_ref.shape[0]
    sub = min(ROW_SUB, seq)
    nt = (((1,), (1,)), ((), ()))
    lv = lam_ref[...]
    lam = (jnp.exp(jnp.sum(lv[0:1] * lv[1:2], axis=1, keepdims=True))
           - jnp.exp(jnp.sum(lv[2:3] * lv[3:4], axis=1, keepdims=True)) + lambda_init)
    row = lax.broadcasted_iota(jnp.int32, (sub, sub), 0)
    col = lax.broadcasted_iota(jnp.int32, (sub, sub), 1)
    causal = col <= row

    chains = [(r0, c0) for r0 in reversed(range(0, seq, sub)) for c0 in (0, hd)]

    def scores(i):
        r0, c0 = chains[i]
        kv_len = r0 + sub
        s = lax.dot_general(q_ref[r0:r0 + sub, c0:c0 + hd], k_ref[0:kv_len, c0:c0 + hd], nt,
                            preferred_element_type=F32)
        s_diag = jnp.where(causal, s[:, kv_len - sub:], MASK_VALUE)
        return s_diag if kv_len == sub else jnp.concatenate([s[:, :kv_len - sub], s_diag], axis=1)

    def softmax(s):
        p = jnp.exp(s - jnp.max(s, axis=1, keepdims=True))
        return p.astype(BF16), jnp.sum(p, axis=1, keepdims=True)

    def weighted_v(i, p, l):
        return _dot(p, v_ref[0:chains[i][0] + sub, :]) * (1.0 / l)

    s_vals, p_vals, o_vals = {}, {}, {}
    for t in range(len(chains) + ATTN_PV_LAG):
        if t < len(chains):
            s_vals[t] = scores(t)
        if 0 <= t - ATTN_SOFTMAX_LAG < len(chains):
            p_vals[t - ATTN_SOFTMAX_LAG] = softmax(s_vals.pop(t - ATTN_SOFTMAX_LAG))
        if 0 <= t - ATTN_PV_LAG < len(chains):
            i = t - ATTN_PV_LAG
            o_vals[i] = weighted_v(i, *p_vals.pop(i))
            if i % 2 == 1:
                r0 = chains[i][0]
                o = o_vals.pop(i - 1) - lam * o_vals.pop(i)
                o_ref[r0:r0 + sub, :] = (_rms(o, g_ref[...]) * (1.0 - lambda_init)).astype(o_ref.dtype)


def _diff_attention(q, kv, lam_vecs, g_sub, *, batch, heads, hd, lambda_init, side_jobs=()):
    t, d = q.shape
    seq = t // batch
    head_block = pl.BlockSpec((seq, 2 * hd), lambda b, h: (b, h))
    return _host_call(
        functools.partial(_attn_kernel, hd=hd, lambda_init=lambda_init),
        (q, kv, kv, lam_vecs, g_sub.reshape(1, 2 * hd)),
        grid=(batch, heads),
        in_specs=[
            head_block,
            head_block,
            pl.BlockSpec((seq, 2 * hd), lambda b, h: (b, heads + h)),
            pl.BlockSpec((8, hd), lambda b, h: (0, 0)),
            pl.BlockSpec((1, 2 * hd), lambda b, h: (0, 0)),
        ],
        out_spec=head_block,
        out_shape=jax.ShapeDtypeStruct((t, d), BF16),
        side_jobs=side_jobs,
        compiler_params=_params("parallel", "parallel"),
        name="diff_attn",
    )


def _cast_kernel(w_ref, o_ref):
    if len(o_ref.shape) == 2:
        o_ref[...] = w_ref[...].astype(o_ref.dtype)
    else:
        tile = o_ref.shape[2]
        for c in range(o_ref.shape[0]):
            o_ref[c] = w_ref[:, c * tile:(c + 1) * tile].astype(o_ref.dtype)


def _to_bf16(w, layer=None, col_tile=None):
    if layer is None:
        w, layer = w[None], 0
    _, k, n = w.shape
    rows = max(8, min(k, CAST_BLOCK_ELEMS // n))
    assert k % rows == 0
    if col_tile is None:
        out_spec = pl.BlockSpec((rows, n), lambda i: (i, 0))
        out_shape = jax.ShapeDtypeStruct((k, n), BF16)
    else:
        out_spec = pl.BlockSpec((n // col_tile, rows, col_tile), lambda i: (0, i, 0))
        out_shape = jax.ShapeDtypeStruct((n // col_tile, k, col_tile), BF16)
    return pl.pallas_call(
        _cast_kernel,
        grid=(k // rows,),
        in_specs=[pl.BlockSpec((None, rows, n), lambda i: (layer, i, 0))],
        out_specs=out_spec,
        out_shape=out_shape,
        compiler_params=_params("parallel"),
        name="cast_bf16",
    )(w)


def _side_casts(jobs, grid):
    steps = math.prod(grid)

    def flat(*ids):
        i = ids[0]
        for extent, idx in zip(grid[1:], ids[1:]):
            i = i * extent + idx
        return i

    args, in_specs, out_specs, out_shapes = [], [], [], []
    for w, layer, col_tile in jobs:
        _, k, n = w.shape
        rows = k // steps
        if k % steps or rows % 16:
            return None
        args.append(w)
        in_specs.append(pl.BlockSpec((None, rows, n), lambda *ids, layer=layer: (layer, flat(*ids), 0)))
        if col_tile is None:
            out_specs.append(pl.BlockSpec((rows, n), lambda *ids: (flat(*ids), 0)))
            out_shapes.append(jax.ShapeDtypeStruct((k, n), BF16))
        else:
            out_specs.append(pl.BlockSpec((n // col_tile, rows, col_tile), lambda *ids: (0, flat(*ids), 0)))
            out_shapes.append(jax.ShapeDtypeStruct((n // col_tile, k, col_tile), BF16))
    return args, in_specs, out_specs, out_shapes


def _with_side_casts(kernel, n_in, n_side):
    def wrapped(*refs):
        main_in, side_in = refs[:n_in], refs[n_in:n_in + n_side]
        out = refs[n_in + n_side]
        side_out = refs[n_in + n_side + 1:n_in + 2 * n_side + 1]
        for w_ref, o_ref in zip(side_in, side_out):
            _cast_kernel(w_ref, o_ref)
        kernel(*main_in, out, *refs[n_in + 2 * n_side + 1:])

    return wrapped


def _host_call(kernel, args, *, grid, in_specs, out_spec, out_shape, side_jobs=(), **kw):
    side = _side_casts(side_jobs, grid) if side_jobs else None
    if side is None:
        out = pl.pallas_call(kernel, grid=grid, in_specs=in_specs, out_specs=out_spec, out_shape=out_shape,
                             **kw)(*args)
        return out, [_to_bf16(w, layer, col_tile) for w, layer, col_tile in side_jobs]
    s_args, s_in, s_out, s_shapes = side
    outs = pl.pallas_call(
        _with_side_casts(kernel, len(args), len(s_args)), grid=grid, in_specs=list(in_specs) + s_in,
        out_specs=[out_spec] + s_out, out_shape=[out_shape] + s_shapes, **kw)(*args, *s_args)
    return outs[0], list(outs[1:])


def _rope_tables(seq, hd):
    pos = jnp.arange(seq, dtype=F32)
    inv_freq = 1.0 / (ROPE_THETA ** (jnp.arange(0, hd, 2, dtype=F32) / hd))
    ang = pos[:, None] * inv_freq[None, :]
    emb = jnp.concatenate([ang, ang], axis=-1)
    sign = jnp.where(jnp.arange(hd) < hd // 2, -1.0, 1.0).astype(F32)
    return jnp.cos(emb), jnp.sin(emb) * sign


def kernel(x, mix_pre_g, mix_post_g, mlp_pre_g, mlp_post_g, ssm_w_in, ssm_a_re, ssm_a_im, ssm_log_dt, ssm_b_re, ssm_b_im, ssm_c_re, ssm_c_im, ssm_d, ssm_w_glu, kv_norm_g, w_kv, attn_w_q, lam_q1, lam_k1, lam_q2, lam_k2, attn_subln_g, attn_w_o, mlp_w_up, mlp_w_down):
    batch, seq, d = x.shape
    t = batch * seq
    depth = mix_pre_g.shape[0]
    n_a = ssm_w_in.shape[0]
    hd = lam_q1.shape[1]
    heads = d // (2 * hd)
    cos, sin = _rope_tables(seq, hd)

    h = x.reshape(t, d)
    kv = w_kv_bf = w_q_first = None
    bf = min(MLP_HIDDEN_TILE, mlp_w_up.shape[2])
    for l in range(depth):
        up_job, down_job = (mlp_w_up, l, bf), (mlp_w_down, l, None)
        if l < n_a:
            a = l
            u, (w_glu,) = _norm_matmul(h, mix_pre_g[l], _to_bf16(ssm_w_in, a), out_dtype=F32,
                                       bm=ROWS["ssm_in_proj"], name="ssm_in_proj",
                                       side_jobs=((ssm_w_glu, a, None),))
            sw = _ssm_weights(ssm_a_re[a], ssm_a_im[a], ssm_log_dt[a], ssm_b_re[a], ssm_b_im[a],
                              ssm_c_re[a], ssm_c_im[a], SSM_TC)
            z, (w_up,) = _ssm_scan(u, *sw, ssm_d[a], batch=batch, side_jobs=(up_job,))
            glu_jobs = [down_job]
            if l == n_a - 1:
                glu_jobs.append((w_kv[None], 0, None))
                if depth > n_a:
                    glu_jobs.append((attn_w_q, 0, None))
            h, casts = _matmul_norm_res(z, w_glu, h, mix_post_g[l], bm=ROWS["ssm_glu_out"], glu=True,
                                        name="ssm_glu_out", side_jobs=tuple(glu_jobs))
            w_down = casts[0]
            if l == n_a - 1:
                w_kv_bf = casts[1]
                w_q_first = casts[2] if depth > n_a else None
        else:
            b = l - n_a
            lambda_init = 0.8 - 0.6 * math.exp(-0.3 * l)
            w_q = w_q_first if (b == 0 and w_q_first is not None) else _to_bf16(attn_w_q, b)
            q, (w_o,) = _norm_matmul(h, mix_pre_g[l], w_q, out_dtype=BF16, bm=ROWS["q_proj"], name="q_proj",
                                     rope=(cos, sin), rope_cols=d, scale=hd ** -0.5,
                                     side_jobs=((attn_w_o, b, None),))
            lam_vecs = jnp.zeros((8, hd), F32).at[0:4].set(
                jnp.stack([lam_q1[b], lam_k1[b], lam_q2[b], lam_k2[b]]).astype(F32))
            o, (w_up, w_down) = _diff_attention(q, kv, lam_vecs, attn_subln_g[b], batch=batch, heads=heads, hd=hd,
                                                lambda_init=lambda_init, side_jobs=(up_job, down_job))
            h, _ = _matmul_norm_res(o, w_o, h, mix_post_g[l], bm=ROWS["attn_out_proj"], name="attn_out_proj")
        h = _mlp(h, mlp_pre_g[l], w_up, w_down, mlp_post_g[l], bm=ROWS["mlp"])
        if l == n_a - 1:
            kv, _ = _norm_matmul(h, kv_norm_g, w_kv_bf, out_dtype=BF16, bm=ROWS["kv_proj"], name="kv_proj",
                                 rope=(cos, sin), rope_cols=d, scale=1.0)
    return h.reshape(batch, seq, d)
```

```python
import functools
import math

import jax
import jax.numpy as jnp
from jax import lax
from jax.experimental import pallas as pl
from jax.experimental.pallas import tpu as pltpu

EPS = 1e-6
ROPE_THETA = 10000.0
LANES = 128
SUBLANES = 8
BF16_SUBLANES = 16
SSM_TC = 8
SSM_SEQS = 4
VMEM_LIMIT_BYTES = 56 * 1024 * 1024
MLP_VMEM_LIMIT_BYTES = 58 * 1024 * 1024
MASK_VALUE = -1e30
MLP_DOWN_COLS = 512
ROW_SUB = 256
COL_SLAB = 512
CAST_BLOCK_ELEMS = 2 * 1024 * 1024
ROWS = {"ssm_in_proj": 1024, "ssm_glu_out": 512, "kv_proj": 1024, "q_proj": 1024, "attn_out_proj": 1024,
        "mlp": 1024}
MLP_HIDDEN_TILE = 1024
ATTN_SOFTMAX_LAG = 1
ATTN_PV_LAG = 2

F32 = jnp.float32
BF16 = jnp.bfloat16


def _params(*sem, vmem_limit_bytes=VMEM_LIMIT_BYTES):
    return pltpu.CompilerParams(dimension_semantics=sem, vmem_limit_bytes=vmem_limit_bytes)


def _rms(x, g):
    return x * lax.rsqrt(jnp.mean(x * x, axis=-1, keepdims=True) + EPS) * g


def _dot(a, b):
    return jnp.dot(a, b, preferred_element_type=F32)


def _rope(t, cos, sin_signed):
    return t * cos + pltpu.roll(t, LANES // 2, 1) * sin_signed


def _make_norm_matmul_kernel(rope_cols, scale):
    def kernel(x_ref, g_ref, w_ref, *rest):
        if rope_cols:
            cos_ref, sin_ref, o_ref = rest
        else:
            (o_ref,) = rest
        bm, n = o_ref.shape
        sub, slab = min(ROW_SUB, bm), min(COL_SLAB, n)
        g = g_ref[...]
        for r in range(bm // sub):
            rs = slice(r * sub, (r + 1) * sub)
            xn = _rms(x_ref[rs, :], g).astype(BF16)
            for c in range(n // slab):
                acc = _dot(xn, w_ref[:, c * slab:(c + 1) * slab])
                if c * slab < rope_cols:
                    cos, sin = cos_ref[rs, :], sin_ref[rs, :]
                    for cc in range(slab // LANES):
                        sl = slice(cc * LANES, (cc + 1) * LANES)
                        osl = slice(c * slab + cc * LANES, c * slab + (cc + 1) * LANES)
                        o_ref[rs, osl] = (_rope(acc[:, sl], cos, sin) * scale).astype(o_ref.dtype)
                else:
                    o_ref[rs, c * slab:(c + 1) * slab] = acc.astype(o_ref.dtype)

    return kernel


def _norm_matmul(x, g, w, *, out_dtype, bm, name, rope=None, rope_cols=0, scale=1.0, side_jobs=()):
    m, k = x.shape
    n = w.shape[1]
    bm = min(bm, m)
    args = [x, g.reshape(1, k), w]
    if rope is not None:
        cos, sin = rope
        seq = cos.shape[0]
        bm = min(bm, seq)
        assert rope_cols % min(COL_SLAB, n) == 0 and seq % bm == 0
        nseq = seq // bm
        args += [cos, sin]
    in_specs = [
        pl.BlockSpec((bm, k), lambda i: (i, 0)),
        pl.BlockSpec((1, k), lambda i: (0, 0)),
        pl.BlockSpec((k, n), lambda i: (0, 0), pipeline_mode=pl.Buffered(1)),
    ]
    if rope is not None:
        in_specs += [pl.BlockSpec((bm, LANES), lambda i: (i % nseq, 0))] * 2
    return _host_call(
        _make_norm_matmul_kernel(rope_cols, scale),
        args,
        grid=(m // bm,),
        in_specs=in_specs,
        out_spec=pl.BlockSpec((bm, n), lambda i: (i, 0)),
        out_shape=jax.ShapeDtypeStruct((m, n), out_dtype),
        side_jobs=side_jobs,
        compiler_params=_params("parallel"),
        name=name,
    )


def _ssm_prep_kernel(pw_ref, bb_ref, cc_ref, wi_ref, wn_ref, wo_ref, *, tc, p_ch, n_st):
    s_dim = pw_ref.shape[1]
    gpb = LANES // p_ch
    row_g = lax.shift_right_logical(lax.broadcasted_iota(jnp.int32, (LANES, s_dim), 0), int(math.log2(p_ch)))
    col_g = lax.shift_right_logical(lax.broadcasted_iota(jnp.int32, (LANES, s_dim), 1), int(math.log2(n_st)))
    same_group = row_g == col_g

    def expand(x):
        return jnp.where(same_group, jnp.concatenate([x] * gpb, axis=0), 0.0)

    b_r, b_i = expand(bb_ref[0]), expand(bb_ref[1])
    c_r, c_i = expand(cc_ref[0]), expand(cc_ref[1])
    def split(x):
        hi = x.astype(BF16)
        return hi, (x - hi.astype(F32)).astype(BF16)

    c0_hi, c0_lo = split(jnp.concatenate([c_r, -c_i], axis=1).T)
    kd = []
    for d in range(tc):
        pr, pi = pw_ref[d:d + 1, :], pw_ref[tc + 1 + d:tc + 2 + d, :]
        e_hi, e_lo = split(jnp.concatenate([pr * b_r - pi * b_i, pr * b_i + pi * b_r], axis=1))
        wn_ref[(tc - 1 - d) * LANES:(tc - d) * LANES, :] = e_hi
        kd.append((_dot(e_hi, c0_hi) + (_dot(e_lo, c0_hi) + _dot(e_hi, c0_lo))).astype(BF16))
    for t in range(tc):
        pr, pi = pw_ref[t + 1:t + 2, :], pw_ref[tc + 2 + t:tc + 3 + t, :]
        e = jnp.concatenate([pr * c_r - pi * c_i, -(pr * c_i + pi * c_r)], axis=1)
        wo_ref[:, t * LANES:(t + 1) * LANES] = e.T.astype(BF16)
    zero = jnp.zeros((LANES, LANES), BF16)
    for s in range(tc):
        for t in range(tc):
            wi_ref[s * LANES:(s + 1) * LANES, t * LANES:(t + 1) * LANES] = kd[t - s] if t >= s else zero


def _ssm_weights(a_re, a_im, log_dt, b_re, b_im, c_re, c_im, tc):
    g_n, n_st = a_re.shape
    p_ch = b_re.shape[-1]
    gpb = LANES // p_ch
    nj = g_n // gpb
    s_dim = gpb * n_st
    assert p_ch & (p_ch - 1) == 0 and n_st & (n_st - 1) == 0
    step = jnp.exp(log_dt.astype(F32))[:, None]
    lam_re = jnp.minimum(a_re.astype(F32), -1e-4)
    lam_im = a_im.astype(F32)
    mag = jnp.exp(step * lam_re)
    abar_re = mag * jnp.cos(step * lam_im)
    abar_im = mag * jnp.sin(step * lam_im)
    den = lam_re * lam_re + lam_im * lam_im
    nr = abar_re - 1.0
    ni = abar_im
    coef_re = (nr * lam_re + ni * lam_im) / den
    coef_im = (ni * lam_re - nr * lam_im) / den
    bre, bim = b_re.astype(F32), b_im.astype(F32)
    bbar_re = coef_re[..., None] * bre - coef_im[..., None] * bim
    bbar_im = coef_re[..., None] * bim + coef_im[..., None] * bre
    dd = jnp.arange(tc + 1, dtype=F32)[:, None, None]
    pmag = jnp.exp(dd * (step * lam_re))
    pw = jnp.concatenate([pmag * jnp.cos(dd * (step * lam_im)),
                          pmag * jnp.sin(dd * (step * lam_im))], axis=0)
    pw = pw.reshape(2 * (tc + 1), nj, s_dim).transpose(1, 0, 2)
    bb = jnp.stack([bbar_re, bbar_im]).reshape(2, nj, gpb, n_st, p_ch)
    bb = bb.transpose(1, 0, 4, 2, 3).reshape(nj, 2, p_ch, s_dim)
    cc = jnp.stack([c_re.astype(F32), c_im.astype(F32)]).reshape(2, nj, gpb, p_ch, n_st)
    cc = cc.transpose(1, 0, 3, 2, 4).reshape(nj, 2, p_ch, s_dim)
    k_dim = tc * LANES
    wshape = jax.ShapeDtypeStruct((nj, k_dim, k_dim), BF16)
    assert 2 * s_dim == k_dim
    w_intra, w_in, w_out = pl.pallas_call(
        functools.partial(_ssm_prep_kernel, tc=tc, p_ch=p_ch, n_st=n_st),
        grid=(nj,),
        in_specs=[
            pl.BlockSpec((None, 2 * (tc + 1), s_dim), lambda j: (j, 0, 0)),
            pl.BlockSpec((None, 2, p_ch, s_dim), lambda j: (j, 0, 0, 0)),
            pl.BlockSpec((None, 2, p_ch, s_dim), lambda j: (j, 0, 0, 0)),
        ],
        out_specs=[pl.BlockSpec((None, k_dim, k_dim), lambda j: (j, 0, 0))] * 3,
        out_shape=[wshape] * 3,
        compiler_params=_params("parallel"),
        name="ssm_prep",
    )(pw, bb, cc)
    a_step = jnp.stack([pw[:, tc], pw[:, 2 * tc + 1]], axis=1)
    return w_intra, w_in, w_out, a_step


def _gelu_tanh(x):
    return 0.5 * x * (1.0 + jnp.tanh(math.sqrt(2.0 / math.pi) * (x + 0.044715 * (x * x * x))))


def _ssm_kernel(u_ref, wi_ref, wn_ref, wo_ref, a_ref, d_ref, z_ref, st_ref, zs_ref, *, nb, nc):
    tc = SSM_TC
    rows = nb * nc
    nk = st_ref.shape[0] // 2

    def step_rows(s):
        return pl.ds(s, rows, stride=tc)

    lhs = jnp.concatenate([u_ref[step_rows(s), :].astype(BF16) for s in range(tc)], axis=1)
    contrib = _dot(lhs, wn_ref[...])
    for k in range(2 * nk):
        for b in range(nb):
            st_ref[k, pl.ds(b, nc, stride=nb), :] = contrib[b * nc:(b + 1) * nc, k * LANES:(k + 1) * LANES]

    a = a_ref[...]
    ar = [jnp.broadcast_to(a[0:1, k * LANES:(k + 1) * LANES], (nb, LANES)) for k in range(nk)]
    ai = [jnp.broadcast_to(a[1:2, k * LANES:(k + 1) * LANES], (nb, LANES)) for k in range(nk)]

    def step(c, carry):
        rws = pl.ds(pl.multiple_of(c * nb, nb), nb)
        out = []
        for k in range(nk):
            sr, si = carry[2 * k], carry[2 * k + 1]
            cr = st_ref[k, rws, :]
            ci = st_ref[nk + k, rws, :]
            st_ref[k, rws, :] = sr
            st_ref[nk + k, rws, :] = si
            out += [ar[k] * sr - ai[k] * si + cr, ar[k] * si + ai[k] * sr + ci]
        return tuple(out)

    y_intra = _dot(lhs, wi_ref[...])
    zero = jnp.zeros((nb, LANES), F32)
    lax.fori_loop(0, nc, step, (zero,) * (2 * nk), unroll=True)

    state = jnp.concatenate(
        [jnp.concatenate([st_ref[k, pl.ds(b, nc, stride=nb), :] for b in range(nb)], axis=0).astype(BF16)
         for k in range(2 * nk)], axis=1)
    y = y_intra + _dot(state, wo_ref[...])
    d = d_ref[...]
    for t in range(tc):
        yt = y[:, t * LANES:(t + 1) * LANES] + d * u_ref[step_rows(t), :]
        zs_ref[step_rows(t), :] = _gelu_tanh(yt)
    z_ref[...] = zs_ref[...].astype(z_ref.dtype)


def _ssm_scan(u, w_intra, w_in, w_out, a_step, d_skip, *, batch, side_jobs=()):
    t, d = u.shape
    seq = t // batch
    tc = SSM_TC
    nc = seq // tc
    nb = min(batch, SSM_SEQS)
    k_dim = tc * LANES
    return _host_call(
        functools.partial(_ssm_kernel, nb=nb, nc=nc),
        (u, w_intra, w_in, w_out, a_step, d_skip.reshape(1, d)),
        grid=(d // LANES, batch // nb),
        in_specs=[
            pl.BlockSpec((nb * seq, LANES), lambda j, b: (b, j)),
            pl.BlockSpec((None, k_dim, k_dim), lambda j, b: (j, 0, 0)),
            pl.BlockSpec((None, k_dim, k_dim), lambda j, b: (j, 0, 0)),
            pl.BlockSpec((None, k_dim, k_dim), lambda j, b: (j, 0, 0)),
            pl.BlockSpec((None, 2, k_dim // 2), lambda j, b: (j, 0, 0)),
            pl.BlockSpec((1, LANES), lambda j, b: (0, j)),
        ],
        out_spec=pl.BlockSpec((nb * seq, LANES), lambda j, b: (b, j)),
        out_shape=jax.ShapeDtypeStruct((t, d), BF16),
        side_jobs=side_jobs,
        scratch_shapes=[pltpu.VMEM((k_dim // LANES, nc * nb, LANES), F32),
                        pltpu.VMEM((nb * seq, LANES), F32)],
        compiler_params=_params("parallel", "parallel"),
        name="ssm_scan",
    )


def _make_matmul_norm_res_kernel(glu):
    def kernel(a_ref, w_ref, x_ref, g_ref, o_ref):
        bm, d = o_ref.shape
        sub, slab = min(ROW_SUB, bm), min(COL_SLAB, d)
        g = g_ref[...]
        for r in range(bm // sub):
            rs = slice(r * sub, (r + 1) * sub)
            a = a_ref[rs, :]
            ssq = jnp.zeros((sub, 1), F32)
            for c in range(d // slab):
                cs = slice(c * slab, (c + 1) * slab)
                mix = _dot(a, w_ref[:, cs])
                if glu:
                    gate = _dot(a, w_ref[:, d + c * slab:d + (c + 1) * slab])
                    mix = mix * (1.0 / (1.0 + jnp.exp(-gate)))
                o_ref[rs, cs] = mix
                ssq = ssq + jnp.sum(mix * mix, axis=1, keepdims=True)
            o_ref[rs, :] = x_ref[rs, :] + o_ref[rs, :] * lax.rsqrt(ssq * (1.0 / d) + EPS) * g

    return kernel


def _matmul_norm_res(a, w, x, g, *, bm, name, glu=False, side_jobs=()):
    m, k = a.shape
    d = x.shape[1]
    bm = min(bm, m)
    return _host_call(
        _make_matmul_norm_res_kernel(glu),
        (a, w, x, g.reshape(1, d)),
        grid=(m // bm,),
        in_specs=[
            pl.BlockSpec((bm, k), lambda i: (i, 0)),
            pl.BlockSpec(w.shape, lambda i: (0, 0), pipeline_mode=pl.Buffered(1)),
            pl.BlockSpec((bm, d), lambda i: (i, 0)),
            pl.BlockSpec((1, d), lambda i: (0, 0)),
        ],
        out_spec=pl.BlockSpec((bm, d), lambda i: (i, 0)),
        out_shape=jax.ShapeDtypeStruct((m, d), F32),
        side_jobs=side_jobs,
        compiler_params=_params("parallel"),
        name=name,
    )


def _mlp_kernel(x_ref, gpre_ref, wu_ref, wd_ref, gpost_ref, o_ref, xn_ref, *, nf):
    f = pl.program_id(1)
    bm, d = o_ref.shape
    bn = min(MLP_DOWN_COLS, d)

    def up_down(xn, rs, first):
        a = jnp.square(jnp.maximum(_dot(xn, wu_ref[...]), 0.0)).astype(BF16)
        for c in range(d // bn):
            sl = slice(c * bn, (c + 1) * bn)
            part = _dot(a, wd_ref[:, sl])
            if first:
                o_ref[rs, sl] = part
            else:
                o_ref[rs, sl] += part

    def edge_step(first, last):
        sub = min(ROW_SUB, bm)
        for r in range(bm // sub):
            rs = slice(r * sub, (r + 1) * sub)
            if first:
                xn = _rms(x_ref[rs, :], gpre_ref[...]).astype(BF16)
                xn_ref[rs, :] = xn
            else:
                xn = xn_ref[rs, :]
            up_down(xn, rs, first)
            if last:
                o_ref[rs, :] = x_ref[rs, :] + _rms(o_ref[rs, :], gpost_ref[...])

    @pl.when(f == 0)
    def _():
        edge_step(True, nf == 1)

    if nf > 2:
        @pl.when(jnp.logical_and(f > 0, f < nf - 1))
        def _():
            up_down(xn_ref[...], slice(None), False)

    if nf > 1:
        @pl.when(f == nf - 1)
        def _():
            edge_step(False, True)


def _mlp(x, g_pre, w_up, w_down, g_post, *, bm):
    m, d = x.shape
    nf, _, bf = w_up.shape
    dff = nf * bf
    bm = min(bm, m)
    return pl.pallas_call(
        functools.partial(_mlp_kernel, nf=dff // bf),
        grid=(m // bm, dff // bf),
        in_specs=[
            pl.BlockSpec((bm, d), lambda i, f: (i, 0), pipeline_mode=pl.Buffered(1)),
            pl.BlockSpec((1, d), lambda i, f: (0, 0)),
            pl.BlockSpec((None, d, bf), lambda i, f: (f, 0, 0)),
            pl.BlockSpec((bf, d), lambda i, f: (f, 0)),
            pl.BlockSpec((1, d), lambda i, f: (0, 0)),
        ],
        out_specs=pl.BlockSpec((bm, d), lambda i, f: (i, 0)),
        out_shape=jax.ShapeDtypeStruct((m, d), F32),
        scratch_shapes=[pltpu.VMEM((bm, d), BF16)],
        compiler_params=_params("parallel", "arbitrary", vmem_limit_bytes=MLP_VMEM_LIMIT_BYTES),
        name="mlp",
    )(x, g_pre.reshape(1, d), w_up, w_down, g_post.reshape(1, d))


def _attn_kernel(q_ref, k_ref, v_ref, lam_ref, g_ref, o_ref, *, hd, lambda_init):
    seq = q_ref.shape[0]
    sub = min(ROW_SUB, seq)
    nt = (((1,), (1,)), ((), ()))
    lv = lam_ref[...]
    lam = (jnp.exp(jnp.sum(lv[0:1] * lv[1:2], axis=1, keepdims=True))
           - jnp.exp(jnp.sum(lv[2:3] * lv[3:4], axis=1, keepdims=True)) + lambda_init)
    row = lax.broadcasted_iota(jnp.int32, (sub, sub), 0)
    col = lax.broadcasted_iota(jnp.int32, (sub, sub), 1)
    causal = col <= row

    chains = [(r0, c0) for r0 in reversed(range(0, seq, sub)) for c0 in (0, hd)]

    def scores(i):
        r0, c0 = chains[i]
        kv_len = r0 + sub
        s = lax.dot_general(q_ref[r0:r0 + sub, c0:c0 + hd], k_ref[0:kv_len, c0:c0 + hd], nt,
                            preferred_element_type=F32)
        s_diag = jnp.where(causal, s[:, kv_len - sub:], MASK_VALUE)
        return s_diag if kv_len == sub else jnp.concatenate([s[:, :kv_len - sub], s_diag], axis=1)

    def softmax(s):
        p = jnp.exp(s - jnp.max(s, axis=1, keepdims=True))
        return p.astype(BF16), jnp.sum(p, axis=1, keepdims=True)

    def weighted_v(i, p, l):
        return _dot(p, v_ref[0:chains[i][0] + sub, :]) * (1.0 / l)

    s_vals, p_vals, o_vals = {}, {}, {}
    for t in range(len(chains) + ATTN_PV_LAG):
        if t < len(chains):
            s_vals[t] = scores(t)
        if 0 <= t - ATTN_SOFTMAX_LAG < len(chains):
            p_vals[t - ATTN_SOFTMAX_LAG] = softmax(s_vals.pop(t - ATTN_SOFTMAX_LAG))
        if 0 <= t - ATTN_PV_LAG < len(chains):
            i = t - ATTN_PV_LAG
            o_vals[i] = weighted_v(i, *p_vals.pop(i))
            if i % 2 == 1:
                r0 = chains[i][0]
                o = o_vals.pop(i - 1) - lam * o_vals.pop(i)
                o_ref[r0:r0 + sub, :] = (_rms(o, g_ref[...]) * (1.0 - lambda_init)).astype(o_ref.dtype)


def _diff_attention(q, kv, lam_vecs, g_sub, *, batch, heads, hd, lambda_init, side_jobs=()):
    t, d = q.shape
    seq = t // batch
    head_block = pl.BlockSpec((seq, 2 * hd), lambda b, h: (b, h))
    return _host_call(
        functools.partial(_attn_kernel, hd=hd, lambda_init=lambda_init),
        (q, kv, kv, lam_vecs, g_sub.reshape(1, 2 * hd)),
        grid=(batch, heads),
        in_specs=[
            head_block,
            head_block,
            pl.BlockSpec((seq, 2 * hd), lambda b, h: (b, heads + h)),
            pl.BlockSpec((SUBLANES, hd), lambda b, h: (0, 0)),
            pl.BlockSpec((1, 2 * hd), lambda b, h: (0, 0)),
        ],
        out_spec=head_block,
        out_shape=jax.ShapeDtypeStruct((t, d), BF16),
        side_jobs=side_jobs,
        compiler_params=_params("parallel", "parallel"),
        name="diff_attn",
    )


def _cast_kernel(w_ref, o_ref):
    if len(o_ref.shape) == 2:
        o_ref[...] = w_ref[...].astype(o_ref.dtype)
    else:
        tile = o_ref.shape[2]
        for c in range(o_ref.shape[0]):
            o_ref[c] = w_ref[:, c * tile:(c + 1) * tile].astype(o_ref.dtype)


def _to_bf16(w, layer=None, col_tile=None):
    if layer is None:
        w, layer = w[None], 0
    _, k, n = w.shape
    rows = max(BF16_SUBLANES, min(k, CAST_BLOCK_ELEMS // n))
    assert k % rows == 0
    if col_tile is None:
        out_spec = pl.BlockSpec((rows, n), lambda i: (i, 0))
        out_shape = jax.ShapeDtypeStruct((k, n), BF16)
    else:
        out_spec = pl.BlockSpec((n // col_tile, rows, col_tile), lambda i: (0, i, 0))
        out_shape = jax.ShapeDtypeStruct((n // col_tile, k, col_tile), BF16)
    return pl.pallas_call(
        _cast_kernel,
        grid=(k // rows,),
        in_specs=[pl.BlockSpec((None, rows, n), lambda i: (layer, i, 0))],
        out_specs=out_spec,
        out_shape=out_shape,
        compiler_params=_params("parallel"),
        name="cast_bf16",
    )(w)


def _side_casts(jobs, grid):
    steps = math.prod(grid)

    def flat(*ids):
        i = ids[0]
        for extent, idx in zip(grid[1:], ids[1:]):
            i = i * extent + idx
        return i

    args, in_specs, out_specs, out_shapes = [], [], [], []
    for w, layer, col_tile in jobs:
        _, k, n = w.shape
        rows = k // steps
        if k % steps or rows % BF16_SUBLANES:
            return None
        args.append(w)
        in_specs.append(pl.BlockSpec((None, rows, n), lambda *ids, layer=layer: (layer, flat(*ids), 0)))
        if col_tile is None:
            out_specs.append(pl.BlockSpec((rows, n), lambda *ids: (flat(*ids), 0)))
            out_shapes.append(jax.ShapeDtypeStruct((k, n), BF16))
        else:
            out_specs.append(pl.BlockSpec((n // col_tile, rows, col_tile), lambda *ids: (0, flat(*ids), 0)))
            out_shapes.append(jax.ShapeDtypeStruct((n // col_tile, k, col_tile), BF16))
    return args, in_specs, out_specs, out_shapes


def _with_side_casts(kernel, n_in, n_side):
    def wrapped(*refs):
        main_in, side_in = refs[:n_in], refs[n_in:n_in + n_side]
        out = refs[n_in + n_side]
        side_out = refs[n_in + n_side + 1:n_in + 2 * n_side + 1]
        for w_ref, o_ref in zip(side_in, side_out):
            _cast_kernel(w_ref, o_ref)
        kernel(*main_in, out, *refs[n_in + 2 * n_side + 1:])

    return wrapped


def _host_call(kernel, args, *, grid, in_specs, out_spec, out_shape, side_jobs=(), **kw):
    side = _side_casts(side_jobs, grid) if side_jobs else None
    if side is None:
        out = pl.pallas_call(kernel, grid=grid, in_specs=in_specs, out_specs=out_spec, out_shape=out_shape,
                             **kw)(*args)
        return out, [_to_bf16(w, layer, col_tile) for w, layer, col_tile in side_jobs]
    s_args, s_in, s_out, s_shapes = side
    outs = pl.pallas_call(
        _with_side_casts(kernel, len(args), len(s_args)), grid=grid, in_specs=list(in_specs) + s_in,
        out_specs=[out_spec] + s_out, out_shape=[out_shape] + s_shapes, **kw)(*args, *s_args)
    return outs[0], list(outs[1:])


def _rope_tables(seq, hd):
    pos = jnp.arange(seq, dtype=F32)
    inv_freq = 1.0 / (ROPE_THETA ** (jnp.arange(0, hd, 2, dtype=F32) / hd))
    ang = pos[:, None] * inv_freq[None, :]
    emb = jnp.concatenate([ang, ang], axis=-1)
    sign = jnp.where(jnp.arange(hd) < hd // 2, -1.0, 1.0).astype(F32)
    return jnp.cos(emb), jnp.sin(emb) * sign


def kernel(x, mix_pre_g, mix_post_g, mlp_pre_g, mlp_post_g, ssm_w_in, ssm_a_re, ssm_a_im, ssm_log_dt, ssm_b_re, ssm_b_im, ssm_c_re, ssm_c_im, ssm_d, ssm_w_glu, kv_norm_g, w_kv, attn_w_q, lam_q1, lam_k1, lam_q2, lam_k2, attn_subln_g, attn_w_o, mlp_w_up, mlp_w_down):
    batch, seq, d = x.shape
    t = batch * seq
    depth = mix_pre_g.shape[0]
    n_a = ssm_w_in.shape[0]
    hd = lam_q1.shape[1]
    heads = d // (2 * hd)
    cos, sin = _rope_tables(seq, hd)

    h = x.reshape(t, d)
    kv = w_kv_bf = w_q_first = None
    bf = min(MLP_HIDDEN_TILE, mlp_w_up.shape[2])
    for l in range(depth):
        up_job, down_job = (mlp_w_up, l, bf), (mlp_w_down, l, None)
        if l < n_a:
            a = l
            u, (w_glu,) = _norm_matmul(h, mix_pre_g[l], _to_bf16(ssm_w_in, a), out_dtype=F32,
                                       bm=ROWS["ssm_in_proj"], name="ssm_in_proj",
                                       side_jobs=((ssm_w_glu, a, None),))
            sw = _ssm_weights(ssm_a_re[a], ssm_a_im[a], ssm_log_dt[a], ssm_b_re[a], ssm_b_im[a],
                              ssm_c_re[a], ssm_c_im[a], SSM_TC)
            z, (w_up,) = _ssm_scan(u, *sw, ssm_d[a], batch=batch, side_jobs=(up_job,))
            glu_jobs = [down_job]
            if l == n_a - 1:
                glu_jobs.append((w_kv[None], 0, None))
                if depth > n_a:
                    glu_jobs.append((attn_w_q, 0, None))
            h, casts = _matmul_norm_res(z, w_glu, h, mix_post_g[l], bm=ROWS["ssm_glu_out"], glu=True,
                                        name="ssm_glu_out", side_jobs=tuple(glu_jobs))
            w_down = casts[0]
            if l == n_a - 1:
                w_kv_bf = casts[1]
                w_q_first = casts[2] if depth > n_a else None
        else:
            b = l - n_a
            lambda_init = 0.8 - 0.6 * math.exp(-0.3 * l)
            w_q = w_q_first if (b == 0 and w_q_first is not None) else _to_bf16(attn_w_q, b)
            q, (w_o,) = _norm_matmul(h, mix_pre_g[l], w_q, out_dtype=BF16, bm=ROWS["q_proj"], name="q_proj",
                                     rope=(cos, sin), rope_cols=d, scale=hd ** -0.5,
                                     side_jobs=((attn_w_o, b, None),))
            lam_vecs = jnp.zeros((SUBLANES, hd), F32).at[0:4].set(
                jnp.stack([lam_q1[b], lam_k1[b], lam_q2[b], lam_k2[b]]).astype(F32))
            o, (w_up, w_down) = _diff_attention(q, kv, lam_vecs, attn_subln_g[b], batch=batch, heads=heads, hd=hd,
                                                lambda_init=lambda_init, side_jobs=(up_job, down_job))
            h, _ = _matmul_norm_res(o, w_o, h, mix_post_g[l], bm=ROWS["attn_out_proj"], name="attn_out_proj")
        h = _mlp(h, mlp_pre_g[l], w_up, w_down, mlp_post_g[l], bm=ROWS["mlp"])
        if l == n_a - 1:
            kv, _ = _norm_matmul(h, kv_norm_g, w_kv_bf, out_dtype=BF16, bm=ROWS["kv_proj"], name="kv_proj",
                                 rope=(cos, sin), rope_cols=d, scale=1.0)
    return h.reshape(batch, seq, d)
```

```python
import functools
import math

import jax
import jax.numpy as jnp
from jax import lax
from jax.experimental import pallas as pl
from jax.experimental.pallas import tpu as pltpu

EPS = 1e-6
ROPE_THETA = 10000.0
LANES = 128
SUBLANES = 8
BF16_SUBLANES = 16
SSM_TC = 8
SSM_SEQS = 4
VMEM_LIMIT_BYTES = 56 * 1024 * 1024
MLP_VMEM_LIMIT_BYTES = 58 * 1024 * 1024
MASK_VALUE = -1e30
MLP_DOWN_COLS = 512
ROW_SUB = 256
COL_SLAB = 512
CAST_BLOCK_ELEMS = 2 * 1024 * 1024
ROWS = {"ssm_in_proj": 1024, "ssm_glu_out": 512, "kv_proj": 1024, "q_proj": 1024, "attn_out_proj": 1024,
        "mlp": 1024}
MLP_HIDDEN_TILE = 1024
ATTN_SOFTMAX_LAG = 1
ATTN_PV_LAG = 2

F32 = jnp.float32
BF16 = jnp.bfloat16


def _params(*sem, vmem_limit_bytes=VMEM_LIMIT_BYTES):
    return pltpu.CompilerParams(dimension_semantics=sem, vmem_limit_bytes=vmem_limit_bytes)


def _rms(x, g):
    return x * lax.rsqrt(jnp.mean(x * x, axis=-1, keepdims=True) + EPS) * g


def _dot(a, b):
    return jnp.dot(a, b, preferred_element_type=F32)


def _rope(t, cos, sin_signed):
    return t * cos + pltpu.roll(t, LANES // 2, 1) * sin_signed


def _make_norm_matmul_kernel(rope_cols, scale):
    def kernel(x_ref, g_ref, w_ref, *rest):
        if rope_cols:
            cos_ref, sin_ref, o_ref = rest
        else:
            (o_ref,) = rest
        bm, n = o_ref.shape
        sub, slab = min(ROW_SUB, bm), min(COL_SLAB, n)
        g = g_ref[...]
        for r in range(bm // sub):
            rs = slice(r * sub, (r + 1) * sub)
            xn = _rms(x_ref[rs, :], g).astype(BF16)
            for c in range(n // slab):
                acc = _dot(xn, w_ref[:, c * slab:(c + 1) * slab])
                if c * slab < rope_cols:
                    cos, sin = cos_ref[rs, :], sin_ref[rs, :]
                    for cc in range(slab // LANES):
                        sl = slice(cc * LANES, (cc + 1) * LANES)
                        osl = slice(c * slab + cc * LANES, c * slab + (cc + 1) * LANES)
                        o_ref[rs, osl] = (_rope(acc[:, sl], cos, sin) * scale).astype(o_ref.dtype)
                else:
                    o_ref[rs, c * slab:(c + 1) * slab] = acc.astype(o_ref.dtype)

    return kernel


def _norm_matmul(x, g, w, *, out_dtype, bm, name, rope=None, rope_cols=0, scale=1.0, side_jobs=()):
    m, k = x.shape
    n = w.shape[1]
    bm = min(bm, m)
    args = [x, g.reshape(1, k), w]
    if rope is not None:
        cos, sin = rope
        seq = cos.shape[0]
        bm = min(bm, seq)
        assert rope_cols % min(COL_SLAB, n) == 0 and seq % bm == 0
        nseq = seq // bm
        args += [cos, sin]
    in_specs = [
        pl.BlockSpec((bm, k), lambda i: (i, 0)),
        pl.BlockSpec((1, k), lambda i: (0, 0)),
        pl.BlockSpec((k, n), lambda i: (0, 0), pipeline_mode=pl.Buffered(1)),
    ]
    if rope is not None:
        in_specs += [pl.BlockSpec((bm, LANES), lambda i: (i % nseq, 0))] * 2
    return _host_call(
        _make_norm_matmul_kernel(rope_cols, scale),
        args,
        grid=(m // bm,),
        in_specs=in_specs,
        out_spec=pl.BlockSpec((bm, n), lambda i: (i, 0)),
        out_shape=jax.ShapeDtypeStruct((m, n), out_dtype),
        side_jobs=side_jobs,
        compiler_params=_params("parallel"),
        name=name,
    )


def _ssm_prep_kernel(pw_ref, bb_ref, cc_ref, wi_ref, wn_ref, wo_ref, *, tc, p_ch, n_st):
    s_dim = pw_ref.shape[1]
    gpb = LANES // p_ch
    row_g = lax.shift_right_logical(lax.broadcasted_iota(jnp.int32, (LANES, s_dim), 0), int(math.log2(p_ch)))
    col_g = lax.shift_right_logical(lax.broadcasted_iota(jnp.int32, (LANES, s_dim), 1), int(math.log2(n_st)))
    same_group = row_g == col_g

    def expand(x):
        return jnp.where(same_group, jnp.concatenate([x] * gpb, axis=0), 0.0)

    b_r, b_i = expand(bb_ref[0]), expand(bb_ref[1])
    c_r, c_i = expand(cc_ref[0]), expand(cc_ref[1])
    def split(x):
        hi = x.astype(BF16)
        return hi, (x - hi.astype(F32)).astype(BF16)

    c0_hi, c0_lo = split(jnp.concatenate([c_r, -c_i], axis=1).T)
    kd = []
    for d in range(tc):
        pr, pi = pw_ref[d:d + 1, :], pw_ref[tc + 1 + d:tc + 2 + d, :]
        e_hi, e_lo = split(jnp.concatenate([pr * b_r - pi * b_i, pr * b_i + pi * b_r], axis=1))
        wn_ref[(tc - 1 - d) * LANES:(tc - d) * LANES, :] = e_hi
        kd.append((_dot(e_hi, c0_hi) + (_dot(e_lo, c0_hi) + _dot(e_hi, c0_lo))).astype(BF16))
    for t in range(tc):
        pr, pi = pw_ref[t + 1:t + 2, :], pw_ref[tc + 2 + t:tc + 3 + t, :]
        e = jnp.concatenate([pr * c_r - pi * c_i, -(pr * c_i + pi * c_r)], axis=1)
        wo_ref[:, t * LANES:(t + 1) * LANES] = e.T.astype(BF16)
    zero = jnp.zeros((LANES, LANES), BF16)
    for s in range(tc):
        for t in range(tc):
            wi_ref[s * LANES:(s + 1) * LANES, t * LANES:(t + 1) * LANES] = kd[t - s] if t >= s else zero


def _ssm_weights(a_re, a_im, log_dt, b_re, b_im, c_re, c_im, tc):
    g_n, n_st = a_re.shape
    p_ch = b_re.shape[-1]
    gpb = LANES // p_ch
    nj = g_n // gpb
    s_dim = gpb * n_st
    assert p_ch & (p_ch - 1) == 0 and n_st & (n_st - 1) == 0
    step = jnp.exp(log_dt.astype(F32))[:, None]
    lam_re = jnp.minimum(a_re.astype(F32), -1e-4)
    lam_im = a_im.astype(F32)
    mag = jnp.exp(step * lam_re)
    abar_re = mag * jnp.cos(step * lam_im)
    abar_im = mag * jnp.sin(step * lam_im)
    den = lam_re * lam_re + lam_im * lam_im
    nr = abar_re - 1.0
    ni = abar_im
    coef_re = (nr * lam_re + ni * lam_im) / den
    coef_im = (ni * lam_re - nr * lam_im) / den
    bre, bim = b_re.astype(F32), b_im.astype(F32)
    bbar_re = coef_re[..., None] * bre - coef_im[..., None] * bim
    bbar_im = coef_re[..., None] * bim + coef_im[..., None] * bre
    dd = jnp.arange(tc + 1, dtype=F32)[:, None, None]
    pmag = jnp.exp(dd * (step * lam_re))
    pw = jnp.concatenate([pmag * jnp.cos(dd * (step * lam_im)),
                          pmag * jnp.sin(dd * (step * lam_im))], axis=0)
    pw = pw.reshape(2 * (tc + 1), nj, s_dim).transpose(1, 0, 2)
    bb = jnp.stack([bbar_re, bbar_im]).reshape(2, nj, gpb, n_st, p_ch)
    bb = bb.transpose(1, 0, 4, 2, 3).reshape(nj, 2, p_ch, s_dim)
    cc = jnp.stack([c_re.astype(F32), c_im.astype(F32)]).reshape(2, nj, gpb, p_ch, n_st)
    cc = cc.transpose(1, 0, 3, 2, 4).reshape(nj, 2, p_ch, s_dim)
    k_dim = tc * LANES
    wshape = jax.ShapeDtypeStruct((nj, k_dim, k_dim), BF16)
    assert 2 * s_dim == k_dim
    w_intra, w_in, w_out = pl.pallas_call(
        functools.partial(_ssm_prep_kernel, tc=tc, p_ch=p_ch, n_st=n_st),
        grid=(nj,),
        in_specs=[
            pl.BlockSpec((None, 2 * (tc + 1), s_dim), lambda j: (j, 0, 0)),
            pl.BlockSpec((None, 2, p_ch, s_dim), lambda j: (j, 0, 0, 0)),
            pl.BlockSpec((None, 2, p_ch, s_dim), lambda j: (j, 0, 0, 0)),
        ],
        out_specs=[pl.BlockSpec((None, k_dim, k_dim), lambda j: (j, 0, 0))] * 3,
        out_shape=[wshape] * 3,
        compiler_params=_params("parallel"),
        name="ssm_prep",
    )(pw, bb, cc)
    a_step = jnp.stack([pw[:, tc], pw[:, 2 * tc + 1]], axis=1)
    return w_intra, w_in, w_out, a_step


def _gelu_tanh(x):
    return 0.5 * x * (1.0 + jnp.tanh(math.sqrt(2.0 / math.pi) * (x + 0.044715 * (x * x * x))))


def _ssm_kernel(u_ref, wi_ref, wn_ref, wo_ref, a_ref, d_ref, z_ref, st_ref, zs_ref, *, nb, nc):
    tc = SSM_TC
    rows = nb * nc
    nk = st_ref.shape[0] // 2

    def step_rows(s):
        return pl.ds(s, rows, stride=tc)

    lhs = jnp.concatenate([u_ref[step_rows(s), :].astype(BF16) for s in range(tc)], axis=1)
    contrib = _dot(lhs, wn_ref[...])
    for k in range(2 * nk):
        for b in range(nb):
            st_ref[k, pl.ds(b, nc, stride=nb), :] = contrib[b * nc:(b + 1) * nc, k * LANES:(k + 1) * LANES]

    a = a_ref[...]
    ar = [jnp.broadcast_to(a[0:1, k * LANES:(k + 1) * LANES], (nb, LANES)) for k in range(nk)]
    ai = [jnp.broadcast_to(a[1:2, k * LANES:(k + 1) * LANES], (nb, LANES)) for k in range(nk)]

    def step(c, carry):
        rws = pl.ds(pl.multiple_of(c * nb, nb), nb)
        out = []
        for k in range(nk):
            sr, si = carry[2 * k], carry[2 * k + 1]
            cr = st_ref[k, rws, :]
            ci = st_ref[nk + k, rws, :]
            st_ref[k, rws, :] = sr
            st_ref[nk + k, rws, :] = si
            out += [ar[k] * sr - ai[k] * si + cr, ar[k] * si + ai[k] * sr + ci]
        return tuple(out)

    y_intra = _dot(lhs, wi_ref[...])
    zero = jnp.zeros((nb, LANES), F32)
    lax.fori_loop(0, nc, step, (zero,) * (2 * nk), unroll=True)

    state = jnp.concatenate(
        [jnp.concatenate([st_ref[k, pl.ds(b, nc, stride=nb), :] for b in range(nb)], axis=0).astype(BF16)
         for k in range(2 * nk)], axis=1)
    y = y_intra + _dot(state, wo_ref[...])
    d = d_ref[...]
    for t in range(tc):
        yt = y[:, t * LANES:(t + 1) * LANES] + d * u_ref[step_rows(t), :]
        zs_ref[step_rows(t), :] = _gelu_tanh(yt)
    z_ref[...] = zs_ref[...].astype(z_ref.dtype)


def _ssm_scan(u, w_intra, w_in, w_out, a_step, d_skip, *, batch, side_jobs=()):
    t, d = u.shape
    seq = t // batch
    tc = SSM_TC
    nc = seq // tc
    nb = min(batch, SSM_SEQS)
    k_dim = tc * LANES
    return _host_call(
        functools.partial(_ssm_kernel, nb=nb, nc=nc),
        (u, w_intra, w_in, w_out, a_step, d_skip.reshape(1, d)),
        grid=(d // LANES, batch // nb),
        in_specs=[
            pl.BlockSpec((nb * seq, LANES), lambda j, b: (b, j)),
            pl.BlockSpec((None, k_dim, k_dim), lambda j, b: (j, 0, 0)),
            pl.BlockSpec((None, k_dim, k_dim), lambda j, b: (j, 0, 0)),
            pl.BlockSpec((None, k_dim, k_dim), lambda j, b: (j, 0, 0)),
            pl.BlockSpec((None, 2, k_dim // 2), lambda j, b: (j, 0, 0)),
            pl.BlockSpec((1, LANES), lambda j, b: (0, j)),
        ],
        out_spec=pl.BlockSpec((nb * seq, LANES), lambda j, b: (b, j)),
        out_shape=jax.ShapeDtypeStruct((t, d), BF16),
        side_jobs=side_jobs,
        scratch_shapes=[pltpu.VMEM((k_dim // LANES, nc * nb, LANES), F32),
                        pltpu.VMEM((nb * seq, LANES), F32)],
        compiler_params=_params("parallel", "parallel"),
        name="ssm_scan",
    )


def _make_matmul_norm_res_kernel(glu):
    def kernel(a_ref, w_ref, x_ref, g_ref, o_ref):
        bm, d = o_ref.shape
        sub, slab = min(ROW_SUB, bm), min(COL_SLAB, d)
        g = g_ref[...]
        for r in range(bm // sub):
            rs = slice(r * sub, (r + 1) * sub)
            a = a_ref[rs, :]
            ssq = jnp.zeros((sub, 1), F32)
            for c in range(d // slab):
                cs = slice(c * slab, (c + 1) * slab)
                mix = _dot(a, w_ref[:, cs])
                if glu:
                    gate = _dot(a, w_ref[:, d + c * slab:d + (c + 1) * slab])
                    mix = mix * (1.0 / (1.0 + jnp.exp(-gate)))
                o_ref[rs, cs] = mix
                ssq = ssq + jnp.sum(mix * mix, axis=1, keepdims=True)
            o_ref[rs, :] = x_ref[rs, :] + o_ref[rs, :] * lax.rsqrt(ssq * (1.0 / d) + EPS) * g

    return kernel


def _matmul_norm_res(a, w, x, g, *, bm, name, glu=False, side_jobs=()):
    m, k = a.shape
    d = x.shape[1]
    bm = min(bm, m)
    return _host_call(
        _make_matmul_norm_res_kernel(glu),
        (a, w, x, g.reshape(1, d)),
        grid=(m // bm,),
        in_specs=[
            pl.BlockSpec((bm, k), lambda i: (i, 0)),
            pl.BlockSpec(w.shape, lambda i: (0, 0), pipeline_mode=pl.Buffered(1)),
            pl.BlockSpec((bm, d), lambda i: (i, 0)),
            pl.BlockSpec((1, d), lambda i: (0, 0)),
        ],
        out_spec=pl.BlockSpec((bm, d), lambda i: (i, 0)),
        out_shape=jax.ShapeDtypeStruct((m, d), F32),
        side_jobs=side_jobs,
        compiler_params=_params("parallel"),
        name=name,
    )


def _mlp_kernel(x_ref, gpre_ref, wu_hbm, wd_hbm, gpost_ref, o_ref, xn_ref, wu_buf, wd_buf, sem, *, nf):
    i = pl.program_id(0)
    bm, d = o_ref.shape
    bf = wu_buf.shape[2]
    bn = min(MLP_DOWN_COLS, d)

    def copies(f, slot):
        return (pltpu.make_async_copy(wu_hbm.at[f], wu_buf.at[slot], sem.at[0, slot]),
                pltpu.make_async_copy(wd_hbm.at[pl.ds(f * bf, bf), :], wd_buf.at[slot], sem.at[1, slot]))

    def start(f, slot):
        for c in copies(f, slot):
            c.start()

    def wait(f, slot):
        for c in copies(f, slot):
            c.wait()

    def up_down(xn, rs, slot, first):
        a = jnp.square(jnp.maximum(_dot(xn, wu_buf[slot]), 0.0)).astype(BF16)
        for c in range(d // bn):
            sl = slice(c * bn, (c + 1) * bn)
            part = _dot(a, wd_buf[slot, :, sl])
            if first:
                o_ref[rs, sl] = part
            else:
                o_ref[rs, sl] += part

    def edge_tile(slot, first, last):
        sub = min(ROW_SUB, bm)
        for r in range(bm // sub):
            rs = slice(r * sub, (r + 1) * sub)
            if first:
                xn = _rms(x_ref[rs, :], gpre_ref[...]).astype(BF16)
                xn_ref[rs, :] = xn
            else:
                xn = xn_ref[rs, :]
            up_down(xn, rs, slot, first)
            if last:
                o_ref[rs, :] = x_ref[rs, :] + _rms(o_ref[rs, :], gpost_ref[...])

    @pl.when(i == 0)
    def _():
        start(0, 0)

    start(1, 1)
    wait(0, 0)
    edge_tile(0, True, False)

    def pair(p, carry):
        f = 2 * p + 1
        start(f + 1, 0)
        wait(f, 1)
        up_down(xn_ref[...], slice(None), 1, False)
        start(f + 2, 1)
        wait(f + 1, 0)
        up_down(xn_ref[...], slice(None), 0, False)
        return carry

    lax.fori_loop(0, (nf - 2) // 2, pair, 0)

    @pl.when(i + 1 < pl.num_programs(0))
    def _():
        start(0, 0)

    wait(nf - 1, 1)
    edge_tile(1, False, True)


def _mlp(x, g_pre, w_up, w_down, g_post, *, bm):
    m, d = x.shape
    nf, _, bf = w_up.shape
    assert nf % 2 == 0, "the two-slot weight ring pairs hidden tiles"
    bm = min(bm, m)
    return pl.pallas_call(
        functools.partial(_mlp_kernel, nf=nf),
        grid=(m // bm,),
        in_specs=[
            pl.BlockSpec((bm, d), lambda i: (i, 0), pipeline_mode=pl.Buffered(1)),
            pl.BlockSpec((1, d), lambda i: (0, 0)),
            pl.BlockSpec(memory_space=pl.ANY),
            pl.BlockSpec(memory_space=pl.ANY),
            pl.BlockSpec((1, d), lambda i: (0, 0)),
        ],
        out_specs=pl.BlockSpec((bm, d), lambda i: (i, 0)),
        out_shape=jax.ShapeDtypeStruct((m, d), F32),
        scratch_shapes=[pltpu.VMEM((bm, d), BF16), pltpu.VMEM((2, d, bf), BF16), pltpu.VMEM((2, bf, d), BF16),
                        pltpu.SemaphoreType.DMA((2, 2))],
        compiler_params=_params("arbitrary", vmem_limit_bytes=MLP_VMEM_LIMIT_BYTES),
        name="mlp",
    )(x, g_pre.reshape(1, d), w_up, w_down, g_post.reshape(1, d))


def _attn_kernel(q_ref, k_ref, v_ref, lam_ref, g_ref, o_ref, *, hd, lambda_init):
    seq = q_ref.shape[0]
    sub = min(ROW_SUB, seq)
    nt = (((1,), (1,)), ((), ()))
    lv = lam_ref[...]
    lam = (jnp.exp(jnp.sum(lv[0:1] * lv[1:2], axis=1, keepdims=True))
           - jnp.exp(jnp.sum(lv[2:3] * lv[3:4], axis=1, keepdims=True)) + lambda_init)
    row = lax.broadcasted_iota(jnp.int32, (sub, sub), 0)
    col = lax.broadcasted_iota(jnp.int32, (sub, sub), 1)
    causal = col <= row

    chains = [(r0, c0) for r0 in reversed(range(0, seq, sub)) for c0 in (0, hd)]

    def scores(i):
        r0, c0 = chains[i]
        kv_len = r0 + sub
        s = lax.dot_general(q_ref[r0:r0 + sub, c0:c0 + hd], k_ref[0:kv_len, c0:c0 + hd], nt,
                            preferred_element_type=F32)
        s_diag = jnp.where(causal, s[:, kv_len - sub:], MASK_VALUE)
        return s_diag if kv_len == sub else jnp.concatenate([s[:, :kv_len - sub], s_diag], axis=1)

    def softmax(s):
        p = jnp.exp(s - jnp.max(s, axis=1, keepdims=True))
        return p.astype(BF16), jnp.sum(p, axis=1, keepdims=True)

    def weighted_v(i, p, l):
        return _dot(p, v_ref[0:chains[i][0] + sub, :]) * (1.0 / l)

    s_vals, p_vals, o_vals = {}, {}, {}
    for t in range(len(chains) + ATTN_PV_LAG):
        if t < len(chains):
            s_vals[t] = scores(t)
        if 0 <= t - ATTN_SOFTMAX_LAG < len(chains):
            p_vals[t - ATTN_SOFTMAX_LAG] = softmax(s_vals.pop(t - ATTN_SOFTMAX_LAG))
        if 0 <= t - ATTN_PV_LAG < len(chains):
            i = t - ATTN_PV_LAG
            o_vals[i] = weighted_v(i, *p_vals.pop(i))
            if i % 2 == 1:
                r0 = chains[i][0]
                o = o_vals.pop(i - 1) - lam * o_vals.pop(i)
                o_ref[r0:r0 + sub, :] = (_rms(o, g_ref[...]) * (1.0 - lambda_init)).astype(o_ref.dtype)


def _diff_attention(q, kv, lam_vecs, g_sub, *, batch, heads, hd, lambda_init, side_jobs=()):
    t, d = q.shape
    seq = t // batch
    head_block = pl.BlockSpec((seq, 2 * hd), lambda b, h: (b, h))
    return _host_call(
        functools.partial(_attn_kernel, hd=hd, lambda_init=lambda_init),
        (q, kv, kv, lam_vecs, g_sub.reshape(1, 2 * hd)),
        grid=(batch, heads),
        in_specs=[
            head_block,
            head_block,
            pl.BlockSpec((seq, 2 * hd), lambda b, h: (b, heads + h)),
            pl.BlockSpec((SUBLANES, hd), lambda b, h: (0, 0)),
            pl.BlockSpec((1, 2 * hd), lambda b, h: (0, 0)),
        ],
        out_spec=head_block,
        out_shape=jax.ShapeDtypeStruct((t, d), BF16),
        side_jobs=side_jobs,
        compiler_params=_params("parallel", "parallel"),
        name="diff_attn",
    )


def _cast_kernel(w_ref, o_ref):
    if len(o_ref.shape) == 2:
        o_ref[...] = w_ref[...].astype(o_ref.dtype)
    else:
        tile = o_ref.shape[2]
        for c in range(o_ref.shape[0]):
            o_ref[c] = w_ref[:, c * tile:(c + 1) * tile].astype(o_ref.dtype)


def _to_bf16(w, layer=None, col_tile=None):
    if layer is None:
        w, layer = w[None], 0
    _, k, n = w.shape
    rows = max(BF16_SUBLANES, min(k, CAST_BLOCK_ELEMS // n))
    assert k % rows == 0
    if col_tile is None:
        out_spec = pl.BlockSpec((rows, n), lambda i: (i, 0))
        out_shape = jax.ShapeDtypeStruct((k, n), BF16)
    else:
        out_spec = pl.BlockSpec((n // col_tile, rows, col_tile), lambda i: (0, i, 0))
        out_shape = jax.ShapeDtypeStruct((n // col_tile, k, col_tile), BF16)
    return pl.pallas_call(
        _cast_kernel,
        grid=(k // rows,),
        in_specs=[pl.BlockSpec((None, rows, n), lambda i: (layer, i, 0))],
        out_specs=out_spec,
        out_shape=out_shape,
        compiler_params=_params("parallel"),
        name="cast_bf16",
    )(w)


def _side_casts(jobs, grid):
    steps = math.prod(grid)

    def flat(*ids):
        i = ids[0]
        for extent, idx in zip(grid[1:], ids[1:]):
            i = i * extent + idx
        return i

    args, in_specs, out_specs, out_shapes = [], [], [], []
    for w, layer, col_tile in jobs:
        _, k, n = w.shape
        rows = k // steps
        if k % steps or rows % BF16_SUBLANES:
            return None
        args.append(w)
        in_specs.append(pl.BlockSpec((None, rows, n), lambda *ids, layer=layer: (layer, flat(*ids), 0)))
        if col_tile is None:
            out_specs.append(pl.BlockSpec((rows, n), lambda *ids: (flat(*ids), 0)))
            out_shapes.append(jax.ShapeDtypeStruct((k, n), BF16))
        else:
            out_specs.append(pl.BlockSpec((n // col_tile, rows, col_tile), lambda *ids: (0, flat(*ids), 0)))
            out_shapes.append(jax.ShapeDtypeStruct((n // col_tile, k, col_tile), BF16))
    return args, in_specs, out_specs, out_shapes


def _with_side_casts(kernel, n_in, n_side):
    def wrapped(*refs):
        main_in, side_in = refs[:n_in], refs[n_in:n_in + n_side]
        out = refs[n_in + n_side]
        side_out = refs[n_in + n_side + 1:n_in + 2 * n_side + 1]
        for w_ref, o_ref in zip(side_in, side_out):
            _cast_kernel(w_ref, o_ref)
        kernel(*main_in, out, *refs[n_in + 2 * n_side + 1:])

    return wrapped


def _host_call(kernel, args, *, grid, in_specs, out_spec, out_shape, side_jobs=(), **kw):
    side = _side_casts(side_jobs, grid) if side_jobs else None
    if side is None:
        out = pl.pallas_call(kernel, grid=grid, in_specs=in_specs, out_specs=out_spec, out_shape=out_shape,
                             **kw)(*args)
        return out, [_to_bf16(w, layer, col_tile) for w, layer, col_tile in side_jobs]
    s_args, s_in, s_out, s_shapes = side
    outs = pl.pallas_call(
        _with_side_casts(kernel, len(args), len(s_args)), grid=grid, in_specs=list(in_specs) + s_in,
        out_specs=[out_spec] + s_out, out_shape=[out_shape] + s_shapes, **kw)(*args, *s_args)
    return outs[0], list(outs[1:])


def _rope_tables(seq, hd):
    pos = jnp.arange(seq, dtype=F32)
    inv_freq = 1.0 / (ROPE_THETA ** (jnp.arange(0, hd, 2, dtype=F32) / hd))
    ang = pos[:, None] * inv_freq[None, :]
    emb = jnp.concatenate([ang, ang], axis=-1)
    sign = jnp.where(jnp.arange(hd) < hd // 2, -1.0, 1.0).astype(F32)
    return jnp.cos(emb), jnp.sin(emb) * sign


def kernel(x, mix_pre_g, mix_post_g, mlp_pre_g, mlp_post_g, ssm_w_in, ssm_a_re, ssm_a_im, ssm_log_dt, ssm_b_re, ssm_b_im, ssm_c_re, ssm_c_im, ssm_d, ssm_w_glu, kv_norm_g, w_kv, attn_w_q, lam_q1, lam_k1, lam_q2, lam_k2, attn_subln_g, attn_w_o, mlp_w_up, mlp_w_down):
    batch, seq, d = x.shape
    t = batch * seq
    depth = mix_pre_g.shape[0]
    n_a = ssm_w_in.shape[0]
    hd = lam_q1.shape[1]
    heads = d // (2 * hd)
    cos, sin = _rope_tables(seq, hd)

    h = x.reshape(t, d)
    kv = w_kv_bf = w_q_first = None
    bf = min(MLP_HIDDEN_TILE, mlp_w_up.shape[2])
    for l in range(depth):
        up_job, down_job = (mlp_w_up, l, bf), (mlp_w_down, l, None)
        if l < n_a:
            a = l
            u, (w_glu,) = _norm_matmul(h, mix_pre_g[l], _to_bf16(ssm_w_in, a), out_dtype=F32,
                                       bm=ROWS["ssm_in_proj"], name="ssm_in_proj",
                                       side_jobs=((ssm_w_glu, a, None),))
            sw = _ssm_weights(ssm_a_re[a], ssm_a_im[a], ssm_log_dt[a], ssm_b_re[a], ssm_b_im[a],
                              ssm_c_re[a], ssm_c_im[a], SSM_TC)
            z, (w_up,) = _ssm_scan(u, *sw, ssm_d[a], batch=batch, side_jobs=(up_job,))
            glu_jobs = [down_job]
            if l == n_a - 1:
                glu_jobs.append((w_kv[None], 0, None))
                if depth > n_a:
                    glu_jobs.append((attn_w_q, 0, None))
            h, casts = _matmul_norm_res(z, w_glu, h, mix_post_g[l], bm=ROWS["ssm_glu_out"], glu=True,
                                        name="ssm_glu_out", side_jobs=tuple(glu_jobs))
            w_down = casts[0]
            if l == n_a - 1:
                w_kv_bf = casts[1]
                w_q_first = casts[2] if depth > n_a else None
        else:
            b = l - n_a
            lambda_init = 0.8 - 0.6 * math.exp(-0.3 * l)
            w_q = w_q_first if (b == 0 and w_q_first is not None) else _to_bf16(attn_w_q, b)
            q, (w_o,) = _norm_matmul(h, mix_pre_g[l], w_q, out_dtype=BF16, bm=ROWS["q_proj"], name="q_proj",
                                     rope=(cos, sin), rope_cols=d, scale=hd ** -0.5,
                                     side_jobs=((attn_w_o, b, None),))
            lam_vecs = jnp.zeros((SUBLANES, hd), F32).at[0:4].set(
                jnp.stack([lam_q1[b], lam_k1[b], lam_q2[b], lam_k2[b]]).astype(F32))
            o, (w_up, w_down) = _diff_attention(q, kv, lam_vecs, attn_subln_g[b], batch=batch, heads=heads, hd=hd,
                                                lambda_init=lambda_init, side_jobs=(up_job, down_job))
            h, _ = _matmul_norm_res(o, w_o, h, mix_post_g[l], bm=ROWS["attn_out_proj"], name="attn_out_proj")
        h = _mlp(h, mlp_pre_g[l], w_up, w_down, mlp_post_g[l], bm=ROWS["mlp"])
        if l == n_a - 1:
            kv, _ = _norm_matmul(h, kv_norm_g, w_kv_bf, out_dtype=BF16, bm=ROWS["kv_proj"], name="kv_proj",
                                 rope=(cos, sin), rope_cols=d, scale=1.0)
    return h.reshape(batch, seq, d)
```

```python
import functools
import math

import jax
import jax.numpy as jnp
from jax import lax
from jax.experimental import pallas as pl
from jax.experimental.pallas import tpu as pltpu

EPS = 1e-6
ROPE_THETA = 10000.0
LANES = 128
SUBLANES = 8
BF16_SUBLANES = 16
SSM_TC = 8
SSM_SEQS = 4
VMEM_LIMIT_BYTES = 56 * 1024 * 1024
MLP_VMEM_LIMIT_BYTES = 58 * 1024 * 1024
MASK_VALUE = -1e30
MLP_DOWN_COLS = 512
ROW_SUB = 256
COL_SLAB = 512
CAST_BLOCK_ELEMS = 2 * 1024 * 1024
ROWS = {"ssm_in_proj": 1024, "ssm_glu_out": 512, "kv_proj": 1024, "q_proj": 1024, "attn_out_proj": 1024,
        "mlp": 1024}
MLP_HIDDEN_TILE = 1024
ATTN_SOFTMAX_LAG = 1
ATTN_PV_LAG = 2

F32 = jnp.float32
BF16 = jnp.bfloat16


def _params(*sem, vmem_limit_bytes=VMEM_LIMIT_BYTES):
    return pltpu.CompilerParams(dimension_semantics=sem, vmem_limit_bytes=vmem_limit_bytes)


def _rms(x, g):
    return x * lax.rsqrt(jnp.mean(x * x, axis=-1, keepdims=True) + EPS) * g


def _dot(a, b):
    return jnp.dot(a, b, preferred_element_type=F32)


def _rope(t, cos, sin_signed):
    return t * cos + pltpu.roll(t, LANES // 2, 1) * sin_signed


def _make_norm_matmul_kernel(rope_cols, scale):
    def kernel(x_ref, g_ref, w_ref, *rest):
        if rope_cols:
            cos_ref, sin_ref, o_ref = rest
        else:
            (o_ref,) = rest
        bm, n = o_ref.shape
        sub, slab = min(ROW_SUB, bm), min(COL_SLAB, n)
        g = g_ref[...]
        for r in range(bm // sub):
            rs = slice(r * sub, (r + 1) * sub)
            xn = _rms(x_ref[rs, :], g).astype(BF16)
            for c in range(n // slab):
                acc = _dot(xn, w_ref[:, c * slab:(c + 1) * slab])
                if c * slab < rope_cols:
                    cos, sin = cos_ref[rs, :], sin_ref[rs, :]
                    for cc in range(slab // LANES):
                        sl = slice(cc * LANES, (cc + 1) * LANES)
                        osl = slice(c * slab + cc * LANES, c * slab + (cc + 1) * LANES)
                        o_ref[rs, osl] = (_rope(acc[:, sl], cos, sin) * scale).astype(o_ref.dtype)
                else:
                    o_ref[rs, c * slab:(c + 1) * slab] = acc.astype(o_ref.dtype)

    return kernel


def _norm_matmul(x, g, w, *, out_dtype, bm, name, rope=None, rope_cols=0, scale=1.0, side_jobs=()):
    m, k = x.shape
    n = w.shape[1]
    bm = min(bm, m)
    args = [x, g.reshape(1, k), w]
    if rope is not None:
        cos, sin = rope
        seq = cos.shape[0]
        bm = min(bm, seq)
        assert rope_cols % min(COL_SLAB, n) == 0 and seq % bm == 0
        nseq = seq // bm
        args += [cos, sin]
    in_specs = [
        pl.BlockSpec((bm, k), lambda i: (i, 0)),
        pl.BlockSpec((1, k), lambda i: (0, 0)),
        pl.BlockSpec((k, n), lambda i: (0, 0), pipeline_mode=pl.Buffered(1)),
    ]
    if rope is not None:
        in_specs += [pl.BlockSpec((bm, LANES), lambda i: (i % nseq, 0))] * 2
    return _host_call(
        _make_norm_matmul_kernel(rope_cols, scale),
        args,
        grid=(m // bm,),
        in_specs=in_specs,
        out_spec=pl.BlockSpec((bm, n), lambda i: (i, 0)),
        out_shape=jax.ShapeDtypeStruct((m, n), out_dtype),
        side_jobs=side_jobs,
        compiler_params=_params("parallel"),
        name=name,
    )


def _ssm_prep_kernel(pw_ref, bb_ref, cc_ref, wi_ref, wn_ref, wo_ref, *, tc, p_ch, n_st):
    s_dim = pw_ref.shape[1]
    gpb = LANES // p_ch
    row_g = lax.shift_right_logical(lax.broadcasted_iota(jnp.int32, (LANES, s_dim), 0), int(math.log2(p_ch)))
    col_g = lax.shift_right_logical(lax.broadcasted_iota(jnp.int32, (LANES, s_dim), 1), int(math.log2(n_st)))
    same_group = row_g == col_g

    def expand(x):
        return jnp.where(same_group, jnp.concatenate([x] * gpb, axis=0), 0.0)

    b_r, b_i = expand(bb_ref[0]), expand(bb_ref[1])
    c_r, c_i = expand(cc_ref[0]), expand(cc_ref[1])
    def split(x):
        hi = x.astype(BF16)
        return hi, (x - hi.astype(F32)).astype(BF16)

    c0_hi, c0_lo = split(jnp.concatenate([c_r, -c_i], axis=1).T)
    kd = []
    for d in range(tc):
        pr, pi = pw_ref[d:d + 1, :], pw_ref[tc + 1 + d:tc + 2 + d, :]
        e_hi, e_lo = split(jnp.concatenate([pr * b_r - pi * b_i, pr * b_i + pi * b_r], axis=1))
        wn_ref[(tc - 1 - d) * LANES:(tc - d) * LANES, :] = e_hi
        kd.append((_dot(e_hi, c0_hi) + (_dot(e_lo, c0_hi) + _dot(e_hi, c0_lo))).astype(BF16))
    for t in range(tc):
        pr, pi = pw_ref[t + 1:t + 2, :], pw_ref[tc + 2 + t:tc + 3 + t, :]
        e = jnp.concatenate([pr * c_r - pi * c_i, -(pr * c_i + pi * c_r)], axis=1)
        wo_ref[:, t * LANES:(t + 1) * LANES] = e.T.astype(BF16)
    zero = jnp.zeros((LANES, LANES), BF16)
    for s in range(tc):
        for t in range(tc):
            wi_ref[s * LANES:(s + 1) * LANES, t * LANES:(t + 1) * LANES] = kd[t - s] if t >= s else zero


def _ssm_weights(a_re, a_im, log_dt, b_re, b_im, c_re, c_im, tc):
    g_n, n_st = a_re.shape
    p_ch = b_re.shape[-1]
    gpb = LANES // p_ch
    nj = g_n // gpb
    s_dim = gpb * n_st
    assert p_ch & (p_ch - 1) == 0 and n_st & (n_st - 1) == 0
    step = jnp.exp(log_dt.astype(F32))[:, None]
    lam_re = jnp.minimum(a_re.astype(F32), -1e-4)
    lam_im = a_im.astype(F32)
    mag = jnp.exp(step * lam_re)
    abar_re = mag * jnp.cos(step * lam_im)
    abar_im = mag * jnp.sin(step * lam_im)
    den = lam_re * lam_re + lam_im * lam_im
    nr = abar_re - 1.0
    ni = abar_im
    coef_re = (nr * lam_re + ni * lam_im) / den
    coef_im = (ni * lam_re - nr * lam_im) / den
    bre, bim = b_re.astype(F32), b_im.astype(F32)
    bbar_re = coef_re[..., None] * bre - coef_im[..., None] * bim
    bbar_im = coef_re[..., None] * bim + coef_im[..., None] * bre
    dd = jnp.arange(tc + 1, dtype=F32)[:, None, None]
    pmag = jnp.exp(dd * (step * lam_re))
    pw = jnp.concatenate([pmag * jnp.cos(dd * (step * lam_im)),
                          pmag * jnp.sin(dd * (step * lam_im))], axis=0)
    pw = pw.reshape(2 * (tc + 1), nj, s_dim).transpose(1, 0, 2)
    bb = jnp.stack([bbar_re, bbar_im]).reshape(2, nj, gpb, n_st, p_ch)
    bb = bb.transpose(1, 0, 4, 2, 3).reshape(nj, 2, p_ch, s_dim)
    cc = jnp.stack([c_re.astype(F32), c_im.astype(F32)]).reshape(2, nj, gpb, p_ch, n_st)
    cc = cc.transpose(1, 0, 3, 2, 4).reshape(nj, 2, p_ch, s_dim)
    k_dim = tc * LANES
    wshape = jax.ShapeDtypeStruct((nj, k_dim, k_dim), BF16)
    assert 2 * s_dim == k_dim
    w_intra, w_in, w_out = pl.pallas_call(
        functools.partial(_ssm_prep_kernel, tc=tc, p_ch=p_ch, n_st=n_st),
        grid=(nj,),
        in_specs=[
            pl.BlockSpec((None, 2 * (tc + 1), s_dim), lambda j: (j, 0, 0)),
            pl.BlockSpec((None, 2, p_ch, s_dim), lambda j: (j, 0, 0, 0)),
            pl.BlockSpec((None, 2, p_ch, s_dim), lambda j: (j, 0, 0, 0)),
        ],
        out_specs=[pl.BlockSpec((None, k_dim, k_dim), lambda j: (j, 0, 0))] * 3,
        out_shape=[wshape] * 3,
        compiler_params=_params("parallel"),
        name="ssm_prep",
    )(pw, bb, cc)
    a_step = jnp.stack([pw[:, tc], pw[:, 2 * tc + 1]], axis=1)
    return w_intra, w_in, w_out, a_step


def _gelu_tanh(x):
    return 0.5 * x * (1.0 + jnp.tanh(math.sqrt(2.0 / math.pi) * (x + 0.044715 * (x * x * x))))


def _ssm_kernel(u_ref, wi_ref, wn_ref, wo_ref, a_ref, d_ref, z_ref, st_ref, zs_ref, *, nb, nc):
    tc = SSM_TC
    rows = nb * nc
    nk = st_ref.shape[0] // 2

    def step_rows(s):
        return pl.ds(s, rows, stride=tc)

    lhs = jnp.concatenate([u_ref[step_rows(s), :].astype(BF16) for s in range(tc)], axis=1)
    contrib = _dot(lhs, wn_ref[...])
    for k in range(2 * nk):
        for b in range(nb):
            st_ref[k, pl.ds(b, nc, stride=nb), :] = contrib[b * nc:(b + 1) * nc, k * LANES:(k + 1) * LANES]

    a = a_ref[...]
    ar = [jnp.broadcast_to(a[0:1, k * LANES:(k + 1) * LANES], (nb, LANES)) for k in range(nk)]
    ai = [jnp.broadcast_to(a[1:2, k * LANES:(k + 1) * LANES], (nb, LANES)) for k in range(nk)]

    def step(c, carry):
        rws = pl.ds(pl.multiple_of(c * nb, nb), nb)
        out = []
        for k in range(nk):
            sr, si = carry[2 * k], carry[2 * k + 1]
            cr = st_ref[k, rws, :]
            ci = st_ref[nk + k, rws, :]
            st_ref[k, rws, :] = sr
            st_ref[nk + k, rws, :] = si
            out += [ar[k] * sr - ai[k] * si + cr, ar[k] * si + ai[k] * sr + ci]
        return tuple(out)

    y_intra = _dot(lhs, wi_ref[...])
    zero = jnp.zeros((nb, LANES), F32)
    lax.fori_loop(0, nc, step, (zero,) * (2 * nk), unroll=True)

    state = jnp.concatenate(
        [jnp.concatenate([st_ref[k, pl.ds(b, nc, stride=nb), :] for b in range(nb)], axis=0).astype(BF16)
         for k in range(2 * nk)], axis=1)
    y = y_intra + _dot(state, wo_ref[...])
    d = d_ref[...]
    for t in range(tc):
        yt = y[:, t * LANES:(t + 1) * LANES] + d * u_ref[step_rows(t), :]
        zs_ref[step_rows(t), :] = _gelu_tanh(yt)
    z_ref[...] = zs_ref[...].astype(z_ref.dtype)


def _ssm_scan(u, w_intra, w_in, w_out, a_step, d_skip, *, batch, side_jobs=()):
    t, d = u.shape
    seq = t // batch
    tc = SSM_TC
    nc = seq // tc
    nb = min(batch, SSM_SEQS)
    k_dim = tc * LANES
    return _host_call(
        functools.partial(_ssm_kernel, nb=nb, nc=nc),
        (u, w_intra, w_in, w_out, a_step, d_skip.reshape(1, d)),
        grid=(d // LANES, batch // nb),
        in_specs=[
            pl.BlockSpec((nb * seq, LANES), lambda j, b: (b, j)),
            pl.BlockSpec((None, k_dim, k_dim), lambda j, b: (j, 0, 0)),
            pl.BlockSpec((None, k_dim, k_dim), lambda j, b: (j, 0, 0)),
            pl.BlockSpec((None, k_dim, k_dim), lambda j, b: (j, 0, 0)),
            pl.BlockSpec((None, 2, k_dim // 2), lambda j, b: (j, 0, 0)),
            pl.BlockSpec((1, LANES), lambda j, b: (0, j)),
        ],
        out_spec=pl.BlockSpec((nb * seq, LANES), lambda j, b: (b, j)),
        out_shape=jax.ShapeDtypeStruct((t, d), BF16),
        side_jobs=side_jobs,
        scratch_shapes=[pltpu.VMEM((k_dim // LANES, nc * nb, LANES), F32),
                        pltpu.VMEM((nb * seq, LANES), F32)],
        compiler_params=_params("parallel", "parallel"),
        name="ssm_scan",
    )


def _make_matmul_norm_res_kernel(glu):
    def kernel(a_ref, w_ref, x_ref, g_ref, o_ref):
        bm, d = o_ref.shape
        sub, slab = min(ROW_SUB, bm), min(COL_SLAB, d)
        g = g_ref[...]
        for r in range(bm // sub):
            rs = slice(r * sub, (r + 1) * sub)
            a = a_ref[rs, :]
            ssq = jnp.zeros((sub, 1), F32)
            for c in range(d // slab):
                cs = slice(c * slab, (c + 1) * slab)
                mix = _dot(a, w_ref[:, cs])
                if glu:
                    gate = _dot(a, w_ref[:, d + c * slab:d + (c + 1) * slab])
                    mix = mix * (1.0 / (1.0 + jnp.exp(-gate)))
                o_ref[rs, cs] = mix
                ssq = ssq + jnp.sum(mix * mix, axis=1, keepdims=True)
            o_ref[rs, :] = x_ref[rs, :] + o_ref[rs, :] * lax.rsqrt(ssq * (1.0 / d) + EPS) * g

    return kernel


def _matmul_norm_res(a, w, x, g, *, bm, name, glu=False, side_jobs=()):
    m, k = a.shape
    d = x.shape[1]
    bm = min(bm, m)
    return _host_call(
        _make_matmul_norm_res_kernel(glu),
        (a, w, x, g.reshape(1, d)),
        grid=(m // bm,),
        in_specs=[
            pl.BlockSpec((bm, k), lambda i: (i, 0)),
            pl.BlockSpec(w.shape, lambda i: (0, 0), pipeline_mode=pl.Buffered(1)),
            pl.BlockSpec((bm, d), lambda i: (i, 0)),
            pl.BlockSpec((1, d), lambda i: (0, 0)),
        ],
        out_spec=pl.BlockSpec((bm, d), lambda i: (i, 0)),
        out_shape=jax.ShapeDtypeStruct((m, d), F32),
        side_jobs=side_jobs,
        compiler_params=_params("parallel"),
        name=name,
    )


def _mlp_kernel(x_hbm, gpre_ref, wu_hbm, wd_hbm, gpost_ref, o_ref, xn_ref, wu_buf, wd_buf, sem, x_ref, xsem,
                *, nf):
    i = pl.program_id(0)
    bm, d = o_ref.shape
    bf = wu_buf.shape[2]
    bn = min(MLP_DOWN_COLS, d)

    def copies(f, slot):
        return (pltpu.make_async_copy(wu_hbm.at[f], wu_buf.at[slot], sem.at[0, slot]),
                pltpu.make_async_copy(wd_hbm.at[pl.ds(f * bf, bf), :], wd_buf.at[slot], sem.at[1, slot]))

    def start(f, slot):
        for c in copies(f, slot):
            c.start()

    def wait(f, slot):
        for c in copies(f, slot):
            c.wait()

    def up_down(xn, rs, slot, first):
        a = jnp.square(jnp.maximum(_dot(xn, wu_buf[slot]), 0.0)).astype(BF16)
        for c in range(d // bn):
            sl = slice(c * bn, (c + 1) * bn)
            part = _dot(a, wd_buf[slot, :, sl])
            if first:
                o_ref[rs, sl] = part
            else:
                o_ref[rs, sl] += part

    sub = min(ROW_SUB, bm)

    def x_copy(block, r):
        return pltpu.make_async_copy(x_hbm.at[pl.ds(block * bm + r * sub, sub), :],
                                     x_ref.at[r * sub:(r + 1) * sub, :], xsem.at[r])

    def edge_tile(slot, first, last):
        for r in range(bm // sub):
            rs = slice(r * sub, (r + 1) * sub)
            if first:
                x_copy(i, r).wait()
                xn = _rms(x_ref[rs, :], gpre_ref[...]).astype(BF16)
                xn_ref[rs, :] = xn
            else:
                xn = xn_ref[rs, :]
            up_down(xn, rs, slot, first)
            if last:
                o_ref[rs, :] = x_ref[rs, :] + _rms(o_ref[rs, :], gpost_ref[...])

                @pl.when(i + 1 < pl.num_programs(0))
                def _(r=r):
                    x_copy(i + 1, r).start()

    @pl.when(i == 0)
    def _():
        start(0, 0)
        for r in range(bm // sub):
            x_copy(0, r).start()

    start(1, 1)
    wait(0, 0)
    edge_tile(0, True, False)

    def pair(p, carry):
        f = 2 * p + 1
        start(f + 1, 0)
        wait(f, 1)
        up_down(xn_ref[...], slice(None), 1, False)
        start(f + 2, 1)
        wait(f + 1, 0)
        up_down(xn_ref[...], slice(None), 0, False)
        return carry

    lax.fori_loop(0, (nf - 2) // 2, pair, 0)

    @pl.when(i + 1 < pl.num_programs(0))
    def _():
        start(0, 0)

    wait(nf - 1, 1)
    edge_tile(1, False, True)


def _mlp(x, g_pre, w_up, w_down, g_post, *, bm):
    m, d = x.shape
    nf, _, bf = w_up.shape
    assert nf % 2 == 0, "the two-slot weight ring pairs hidden tiles"
    bm = min(bm, m)
    return pl.pallas_call(
        functools.partial(_mlp_kernel, nf=nf),
        grid=(m // bm,),
        in_specs=[
            pl.BlockSpec(memory_space=pl.ANY),
            pl.BlockSpec((1, d), lambda i: (0, 0)),
            pl.BlockSpec(memory_space=pl.ANY),
            pl.BlockSpec(memory_space=pl.ANY),
            pl.BlockSpec((1, d), lambda i: (0, 0)),
        ],
        out_specs=pl.BlockSpec((bm, d), lambda i: (i, 0)),
        out_shape=jax.ShapeDtypeStruct((m, d), F32),
        scratch_shapes=[pltpu.VMEM((bm, d), BF16), pltpu.VMEM((2, d, bf), BF16), pltpu.VMEM((2, bf, d), BF16),
                        pltpu.SemaphoreType.DMA((2, 2)), pltpu.VMEM((bm, d), F32),
                        pltpu.SemaphoreType.DMA((bm // min(ROW_SUB, bm),))],
        compiler_params=_params("arbitrary", vmem_limit_bytes=MLP_VMEM_LIMIT_BYTES),
        name="mlp",
    )(x, g_pre.reshape(1, d), w_up, w_down, g_post.reshape(1, d))


def _attn_kernel(q_ref, k_ref, v_ref, lam_ref, g_ref, o_ref, *, hd, lambda_init):
    seq = q_ref.shape[0]
    sub = min(ROW_SUB, seq)
    nt = (((1,), (1,)), ((), ()))
    lv = lam_ref[...]
    lam = (jnp.exp(jnp.sum(lv[0:1] * lv[1:2], axis=1, keepdims=True))
           - jnp.exp(jnp.sum(lv[2:3] * lv[3:4], axis=1, keepdims=True)) + lambda_init)
    row = lax.broadcasted_iota(jnp.int32, (sub, sub), 0)
    col = lax.broadcasted_iota(jnp.int32, (sub, sub), 1)
    causal = col <= row

    chains = [(r0, c0) for r0 in reversed(range(0, seq, sub)) for c0 in (0, hd)]

    def scores(i):
        r0, c0 = chains[i]
        kv_len = r0 + sub
        s = lax.dot_general(q_ref[r0:r0 + sub, c0:c0 + hd], k_ref[0:kv_len, c0:c0 + hd], nt,
                            preferred_element_type=F32)
        s_diag = jnp.where(causal, s[:, kv_len - sub:], MASK_VALUE)
        return s_diag if kv_len == sub else jnp.concatenate([s[:, :kv_len - sub], s_diag], axis=1)

    def softmax(s):
        p = jnp.exp(s - jnp.max(s, axis=1, keepdims=True))
        return p.astype(BF16), jnp.sum(p, axis=1, keepdims=True)

    def weighted_v(i, p, l):
        return _dot(p, v_ref[0:chains[i][0] + sub, :]) * (1.0 / l)

    s_vals, p_vals, o_vals = {}, {}, {}
    for t in range(len(chains) + ATTN_PV_LAG):
        if t < len(chains):
            s_vals[t] = scores(t)
        if 0 <= t - ATTN_SOFTMAX_LAG < len(chains):
            p_vals[t - ATTN_SOFTMAX_LAG] = softmax(s_vals.pop(t - ATTN_SOFTMAX_LAG))
        if 0 <= t - ATTN_PV_LAG < len(chains):
            i = t - ATTN_PV_LAG
            o_vals[i] = weighted_v(i, *p_vals.pop(i))
            if i % 2 == 1:
                r0 = chains[i][0]
                o = o_vals.pop(i - 1) - lam * o_vals.pop(i)
                o_ref[r0:r0 + sub, :] = (_rms(o, g_ref[...]) * (1.0 - lambda_init)).astype(o_ref.dtype)


def _diff_attention(q, kv, lam_vecs, g_sub, *, batch, heads, hd, lambda_init, side_jobs=()):
    t, d = q.shape
    seq = t // batch
    head_block = pl.BlockSpec((seq, 2 * hd), lambda b, h: (b, h))
    return _host_call(
        functools.partial(_attn_kernel, hd=hd, lambda_init=lambda_init),
        (q, kv, kv, lam_vecs, g_sub.reshape(1, 2 * hd)),
        grid=(batch, heads),
        in_specs=[
            head_block,
            head_block,
            pl.BlockSpec((seq, 2 * hd), lambda b, h: (b, heads + h)),
            pl.BlockSpec((SUBLANES, hd), lambda b, h: (0, 0)),
            pl.BlockSpec((1, 2 * hd), lambda b, h: (0, 0)),
        ],
        out_spec=head_block,
        out_shape=jax.ShapeDtypeStruct((t, d), BF16),
        side_jobs=side_jobs,
        compiler_params=_params("parallel", "parallel"),
        name="diff_attn",
    )


def _cast_kernel(w_ref, o_ref):
    if len(o_ref.shape) == 2:
        o_ref[...] = w_ref[...].astype(o_ref.dtype)
    else:
        tile = o_ref.shape[2]
        for c in range(o_ref.shape[0]):
            o_ref[c] = w_ref[:, c * tile:(c + 1) * tile].astype(o_ref.dtype)


def _to_bf16(w, layer=None, col_tile=None):
    if layer is None:
        w, layer = w[None], 0
    _, k, n = w.shape
    rows = max(BF16_SUBLANES, min(k, CAST_BLOCK_ELEMS // n))
    assert k % rows == 0
    if col_tile is None:
        out_spec = pl.BlockSpec((rows, n), lambda i: (i, 0))
        out_shape = jax.ShapeDtypeStruct((k, n), BF16)
    else:
        out_spec = pl.BlockSpec((n // col_tile, rows, col_tile), lambda i: (0, i, 0))
        out_shape = jax.ShapeDtypeStruct((n // col_tile, k, col_tile), BF16)
    return pl.pallas_call(
        _cast_kernel,
        grid=(k // rows,),
        in_specs=[pl.BlockSpec((None, rows, n), lambda i: (layer, i, 0))],
        out_specs=out_spec,
        out_shape=out_shape,
        compiler_params=_params("parallel"),
        name="cast_bf16",
    )(w)


def _side_casts(jobs, grid):
    steps = math.prod(grid)

    def flat(*ids):
        i = ids[0]
        for extent, idx in zip(grid[1:], ids[1:]):
            i = i * extent + idx
        return i

    args, in_specs, out_specs, out_shapes = [], [], [], []
    for w, layer, col_tile in jobs:
        _, k, n = w.shape
        rows = k // steps
        if k % steps or rows % BF16_SUBLANES:
            return None
        args.append(w)
        in_specs.append(pl.BlockSpec((None, rows, n), lambda *ids, layer=layer: (layer, flat(*ids), 0)))
        if col_tile is None:
            out_specs.append(pl.BlockSpec((rows, n), lambda *ids: (flat(*ids), 0)))
            out_shapes.append(jax.ShapeDtypeStruct((k, n), BF16))
        else:
            out_specs.append(pl.BlockSpec((n // col_tile, rows, col_tile), lambda *ids: (0, flat(*ids), 0)))
            out_shapes.append(jax.ShapeDtypeStruct((n // col_tile, k, col_tile), BF16))
    return args, in_specs, out_specs, out_shapes


def _with_side_casts(kernel, n_in, n_side):
    def wrapped(*refs):
        main_in, side_in = refs[:n_in], refs[n_in:n_in + n_side]
        out = refs[n_in + n_side]
        side_out = refs[n_in + n_side + 1:n_in + 2 * n_side + 1]
        for w_ref, o_ref in zip(side_in, side_out):
            _cast_kernel(w_ref, o_ref)
        kernel(*main_in, out, *refs[n_in + 2 * n_side + 1:])

    return wrapped


def _host_call(kernel, args, *, grid, in_specs, out_spec, out_shape, side_jobs=(), **kw):
    side = _side_casts(side_jobs, grid) if side_jobs else None
    if side is None:
        out = pl.pallas_call(kernel, grid=grid, in_specs=in_specs, out_specs=out_spec, out_shape=out_shape,
                             **kw)(*args)
        return out, [_to_bf16(w, layer, col_tile) for w, layer, col_tile in side_jobs]
    s_args, s_in, s_out, s_shapes = side
    outs = pl.pallas_call(
        _with_side_casts(kernel, len(args), len(s_args)), grid=grid, in_specs=list(in_specs) + s_in,
        out_specs=[out_spec] + s_out, out_shape=[out_shape] + s_shapes, **kw)(*args, *s_args)
    return outs[0], list(outs[1:])


def _rope_tables(seq, hd):
    pos = jnp.arange(seq, dtype=F32)
    inv_freq = 1.0 / (ROPE_THETA ** (jnp.arange(0, hd, 2, dtype=F32) / hd))
    ang = pos[:, None] * inv_freq[None, :]
    emb = jnp.concatenate([ang, ang], axis=-1)
    sign = jnp.where(jnp.arange(hd) < hd // 2, -1.0, 1.0).astype(F32)
    return jnp.cos(emb), jnp.sin(emb) * sign


def kernel(x, mix_pre_g, mix_post_g, mlp_pre_g, mlp_post_g, ssm_w_in, ssm_a_re, ssm_a_im, ssm_log_dt, ssm_b_re, ssm_b_im, ssm_c_re, ssm_c_im, ssm_d, ssm_w_glu, kv_norm_g, w_kv, attn_w_q, lam_q1, lam_k1, lam_q2, lam_k2, attn_subln_g, attn_w_o, mlp_w_up, mlp_w_down):
    batch, seq, d = x.shape
    t = batch * seq
    depth = mix_pre_g.shape[0]
    n_a = ssm_w_in.shape[0]
    hd = lam_q1.shape[1]
    heads = d // (2 * hd)
    cos, sin = _rope_tables(seq, hd)

    h = x.reshape(t, d)
    kv = w_kv_bf = w_q_first = None
    bf = min(MLP_HIDDEN_TILE, mlp_w_up.shape[2])
    for l in range(depth):
        up_job, down_job = (mlp_w_up, l, bf), (mlp_w_down, l, None)
        if l < n_a:
            a = l
            u, (w_glu,) = _norm_matmul(h, mix_pre_g[l], _to_bf16(ssm_w_in, a), out_dtype=F32,
                                       bm=ROWS["ssm_in_proj"], name="ssm_in_proj",
                                       side_jobs=((ssm_w_glu, a, None),))
            sw = _ssm_weights(ssm_a_re[a], ssm_a_im[a], ssm_log_dt[a], ssm_b_re[a], ssm_b_im[a],
                              ssm_c_re[a], ssm_c_im[a], SSM_TC)
            z, (w_up,) = _ssm_scan(u, *sw, ssm_d[a], batch=batch, side_jobs=(up_job,))
            glu_jobs = [down_job]
            if l == n_a - 1:
                glu_jobs.append((w_kv[None], 0, None))
                if depth > n_a:
                    glu_jobs.append((attn_w_q, 0, None))
            h, casts = _matmul_norm_res(z, w_glu, h, mix_post_g[l], bm=ROWS["ssm_glu_out"], glu=True,
                                        name="ssm_glu_out", side_jobs=tuple(glu_jobs))
            w_down = casts[0]
            if l == n_a - 1:
                w_kv_bf = casts[1]
                w_q_first = casts[2] if depth > n_a else None
        else:
            b = l - n_a
            lambda_init = 0.8 - 0.6 * math.exp(-0.3 * l)
            w_q = w_q_first if (b == 0 and w_q_first is not None) else _to_bf16(attn_w_q, b)
            q, (w_o,) = _norm_matmul(h, mix_pre_g[l], w_q, out_dtype=BF16, bm=ROWS["q_proj"], name="q_proj",
                                     rope=(cos, sin), rope_cols=d, scale=hd ** -0.5,
                                     side_jobs=((attn_w_o, b, None),))
            lam_vecs = jnp.zeros((SUBLANES, hd), F32).at[0:4].set(
                jnp.stack([lam_q1[b], lam_k1[b], lam_q2[b], lam_k2[b]]).astype(F32))
            o, (w_up, w_down) = _diff_attention(q, kv, lam_vecs, attn_subln_g[b], batch=batch, heads=heads, hd=hd,
                                                lambda_init=lambda_init, side_jobs=(up_job, down_job))
            h, _ = _matmul_norm_res(o, w_o, h, mix_post_g[l], bm=ROWS["attn_out_proj"], name="attn_out_proj")
        h = _mlp(h, mlp_pre_g[l], w_up, w_down, mlp_post_g[l], bm=ROWS["mlp"])
        if l == n_a - 1:
            kv, _ = _norm_matmul(h, kv_norm_g, w_kv_bf, out_dtype=BF16, bm=ROWS["kv_proj"], name="kv_proj",
                                 rope=(cos, sin), rope_cols=d, scale=1.0)
    return h.reshape(batch, seq, d)
```
